```python
import math
import jax
import jax.numpy as jnp
from jax import lax
import numpy as np

D_MODEL = 1024
BATCH = 4
SEQ = 8192
DEPTH = 1

SSM_WIDTH = 512
SSM_GROUP = 16
SSM_GROUPS = SSM_WIDTH // SSM_GROUP
SSM_STATE = 64
ATT_HEADS = 8
ATT_HEAD_DIM = 64
ATT_WIDTH = ATT_HEADS * ATT_HEAD_DIM
KV_LATENT = 128
IDX_HEADS = 4
IDX_HEAD_DIM = 64
TOPK_MAX = 256
Q_BLOCK = 128
FFN_HIDDEN = ((-(-8 * D_MODEL // 3) + 255) // 256) * 256
IN_SIZES = (SSM_WIDTH, ATT_WIDTH, KV_LATENT, IDX_HEADS * IDX_HEAD_DIM, IDX_HEAD_DIM, IDX_HEADS, D_MODEL, D_MODEL)
D_IN = sum(IN_SIZES)
EPS = 1e-6

kernel_name = "hybrid_s5_dsa_swiglu_adaln"


def rms_norm(x, g):
    xf = x.astype(jnp.float32)
    y = xf * lax.rsqrt(jnp.mean(xf * xf, axis=-1, keepdims=True) + EPS)
    return (y * g.astype(jnp.float32)).astype(x.dtype)


def split_cols(z, sizes):
    offs = []
    o = 0
    for s in sizes[:-1]:
        o += s
        offs.append(o)
    return jnp.split(z, offs, axis=-1)


def s5_branch(xs, a_re, a_im, log_dt, b_re, b_im, c_re, c_im, d_skip, w_glu, b_glu):
    bsz, seq, _ = xs.shape
    xg = xs.reshape(bsz, seq, SSM_GROUPS, SSM_GROUP)
    dt = jnp.exp(log_dt)[:, None]
    mag = jnp.exp(a_re * dt)
    ang = a_im * dt
    lb_re = mag * jnp.cos(ang)
    lb_im = mag * jnp.sin(ang)
    den = a_re * a_re + a_im * a_im
    coef_re = ((lb_re - 1.0) * a_re + lb_im * a_im) / den
    coef_im = (lb_im * a_re - (lb_re - 1.0) * a_im) / den
    bb_re = coef_re[..., None] * b_re - coef_im[..., None] * b_im
    bb_im = coef_re[..., None] * b_im + coef_im[..., None] * b_re
    bu_re = jnp.einsum('blgc,gpc->blgp', xg, bb_re)
    bu_im = jnp.einsum('blgc,gpc->blgp', xg, bb_im)
    ar = jnp.broadcast_to(lb_re, bu_re.shape)
    ai = jnp.broadcast_to(lb_im, bu_im.shape)

    def combine(e1, e2):
        a1r, a1i, b1r, b1i = e1
        a2r, a2i, b2r, b2i = e2
        nar = a1r * a2r - a1i * a2i
        nai = a1r * a2i + a1i * a2r
        nbr = a2r * b1r - a2i * b1i + b2r
        nbi = a2r * b1i + a2i * b1r + b2i
        return (nar, nai, nbr, nbi)

    _, _, s_re, s_im = lax.associative_scan(combine, (ar, ai, bu_re, bu_im), axis=1)
    y = jnp.einsum('blgp,gcp->blgc', s_re, c_re) - jnp.einsum('blgp,gcp->blgc', s_im, c_im)
    y = y.reshape(bsz, seq, SSM_WIDTH) + d_skip * xs
    y = jax.nn.gelu(y)
    z = y @ w_glu + b_glu
    val, gate = jnp.split(z, 2, axis=-1)
    return val * jax.nn.sigmoid(gate)


def dsa_branch(q, c_kv, q_idx, k_idx, w_idx, w_uk, w_uv):
    bsz, seq = q.shape[0], q.shape[1]
    topk = min(TOPK_MAX, seq // 4)
    nblk = seq // Q_BLOCK
    q_lat = jnp.einsum('blhd,chd->blhc', q, w_uk) * (ATT_HEAD_DIM ** -0.5)
    w_idx = w_idx * (IDX_HEADS ** -0.5)
    q_idx = q_idx * (IDX_HEAD_DIM ** -0.5)

    def to_blocks(t):
        return jnp.moveaxis(t.reshape((bsz, nblk, Q_BLOCK) + t.shape[2:]), 1, 0)

    key_pos = jnp.arange(seq)

    def one_block(args):
        blk, ql, qi, wi = args
        q_pos = blk * Q_BLOCK + jnp.arange(Q_BLOCK)
        rel = jax.nn.relu(jnp.einsum('bqhd,bkd->bqhk', qi, k_idx).astype(jnp.float32))
        iscore = jnp.einsum('bqhk,bqh->bqk', rel, wi.astype(jnp.float32))
        causal = key_pos[None, :] <= q_pos[:, None]
        iscore = jnp.where(causal[None], iscore, -jnp.inf)
        _, sel = lax.top_k(iscore, topk)
        valid = sel <= q_pos[None, :, None]
        c_sel = jax.vmap(lambda cb, ib: cb[ib])(c_kv, sel)
        logits = jnp.einsum('bqhc,bqkc->bhqk', ql, c_sel).astype(jnp.float32)
        logits = jnp.where(valid[:, None], logits, -jnp.inf)
        p = jax.nn.softmax(logits, axis=-1).astype(c_sel.dtype)
        return jnp.einsum('bhqk,bqkc->bqhc', p, c_sel)

    o = lax.map(one_block, (jnp.arange(nblk), to_blocks(q_lat), to_blocks(q_idx), to_blocks(w_idx)))
    o = jnp.moveaxis(o, 0, 1).reshape(bsz, seq, ATT_HEADS, KV_LATENT)
    out = jnp.einsum('blhc,chd->blhd', o, w_uv)
    return out.reshape(bsz, seq, ATT_WIDTH)


def setup_inputs(seed: int = 0) -> dict:
    key = jax.random.key(seed)
    ks = jax.random.split(key, 27)
    f32 = jnp.float32

    def nrm(k, shape, s):
        return jax.random.normal(k, shape, f32) * s

    def gain(k, shape):
        return 1.0 + 0.05 * jax.random.normal(k, shape, f32)

    G, P, Cg = SSM_GROUPS, SSM_STATE, SSM_GROUP
    a_im0 = jnp.broadcast_to(jnp.pi * jnp.arange(P, dtype=f32), (DEPTH, G, P))
    return {
        'x': nrm(ks[0], (BATCH, SEQ, D_MODEL), 1.0),
        'c': nrm(ks[1], (BATCH, D_MODEL), 1.0),
        'w_mod': nrm(ks[2], (DEPTH, D_MODEL, 6 * D_MODEL), 0.5 * D_MODEL ** -0.5),
        'b_mod': nrm(ks[3], (DEPTH, 6 * D_MODEL), 0.01),
        'norm1_g': gain(ks[4], (DEPTH, D_MODEL)),
        'w_in': nrm(ks[5], (DEPTH, D_MODEL, D_IN), D_MODEL ** -0.5),
        'ssm_a_re': -0.5 + nrm(ks[6], (DEPTH, G, P), 0.01),
        'ssm_a_im': a_im0 + nrm(ks[7], (DEPTH, G, P), 0.01),
        'ssm_log_dt': jax.random.uniform(ks[8], (DEPTH, G), f32, math.log(1e-3), math.log(1e-1)),
        'ssm_b_re': nrm(ks[9], (DEPTH, G, P, Cg), (2 * Cg) ** -0.5),
        'ssm_b_im': nrm(ks[10], (DEPTH, G, P, Cg), (2 * Cg) ** -0.5),
        'ssm_c_re': nrm(ks[11], (DEPTH, G, Cg, P), (2 * P) ** -0.5),
        'ssm_c_im': nrm(ks[12], (DEPTH, G, Cg, P), (2 * P) ** -0.5),
        'ssm_d': nrm(ks[13], (DEPTH, SSM_WIDTH), 1.0),
        'w_ssm_glu': nrm(ks[14], (DEPTH, SSM_WIDTH, 2 * D_MODEL), SSM_WIDTH ** -0.5),
        'b_ssm_glu': nrm(ks[15], (DEPTH, 2 * D_MODEL), 0.01),
        'kv_norm_g': gain(ks[16], (DEPTH, KV_LATENT)),
        'idx_k_norm_g': gain(ks[17], (DEPTH, IDX_HEAD_DIM)),
        'w_uk': nrm(ks[18], (DEPTH, KV_LATENT, ATT_HEADS, ATT_HEAD_DIM), KV_LATENT ** -0.5),
        'w_uv': nrm(ks[19], (DEPTH, KV_LATENT, ATT_HEADS, ATT_HEAD_DIM), KV_LATENT ** -0.5),
        'w_attn_proj': nrm(ks[20], (DEPTH, ATT_WIDTH, D_MODEL), ATT_WIDTH ** -0.5),
        'w_out': nrm(ks[21], (DEPTH, D_MODEL, D_MODEL), D_MODEL ** -0.5),
        'norm2_g': gain(ks[22], (DEPTH, D_MODEL)),
        'w_ffn_gate': nrm(ks[23], (DEPTH, D_MODEL, FFN_HIDDEN), D_MODEL ** -0.5),
        'w_ffn_up': nrm(ks[24], (DEPTH, D_MODEL, FFN_HIDDEN), D_MODEL ** -0.5),
        'w_ffn_down': nrm(ks[25], (DEPTH, FFN_HIDDEN, D_MODEL), FFN_HIDDEN ** -0.5),
        'final_g': gain(ks[26], (D_MODEL,)),
    }


def reference(x, c, w_mod, b_mod, norm1_g, w_in, ssm_a_re, ssm_a_im, ssm_log_dt, ssm_b_re, ssm_b_im, ssm_c_re, ssm_c_im, ssm_d, w_ssm_glu, b_ssm_glu, kv_norm_g, idx_k_norm_g, w_uk, w_uv, w_attn_proj, w_out, norm2_g, w_ffn_gate, w_ffn_up, w_ffn_down, final_g):
    bsz, seq, _ = x.shape
    h = x
    cond = jax.nn.silu(c)
    for layer in range(DEPTH):
        mod = cond @ w_mod[layer] + b_mod[layer]
        shift1, scale1, gate1, shift2, scale2, gate2 = jnp.split(mod[:, None, :], 6, axis=-1)
        u = rms_norm(h, norm1_g[layer]) * (1.0 + scale1) + shift1
        xs, q, ckv, qi, ki, wi, ga, gb = split_cols(u @ w_in[layer], IN_SIZES)
        y_ssm = s5_branch(xs, ssm_a_re[layer], ssm_a_im[layer], ssm_log_dt[layer], ssm_b_re[layer], ssm_b_im[layer], ssm_c_re[layer], ssm_c_im[layer], ssm_d[layer], w_ssm_glu[layer], b_ssm_glu[layer])
        y_att = dsa_branch(q.reshape(bsz, seq, ATT_HEADS, ATT_HEAD_DIM), rms_norm(ckv, kv_norm_g[layer]), qi.reshape(bsz, seq, IDX_HEADS, IDX_HEAD_DIM), rms_norm(ki, idx_k_norm_g[layer]), wi, w_uk[layer], w_uv[layer]) @ w_attn_proj[layer]
        merged = jax.nn.sigmoid(ga) * y_ssm + jax.nn.sigmoid(gb) * y_att
        h = h + gate1 * (merged @ w_out[layer])
        u2 = rms_norm(h, norm2_g[layer]) * (1.0 + scale2) + shift2
        ffn = (jax.nn.silu(u2 @ w_ffn_gate[layer]) * (u2 @ w_ffn_up[layer])) @ w_ffn_down[layer]
        h = h + gate2 * ffn
    return rms_norm(h, final_g)
```

```python
import functools

import jax
import jax.numpy as jnp
from jax import lax
from jax.experimental import pallas as pl
from jax.experimental.pallas import tpu as pltpu

F32 = jnp.float32
BF16 = jnp.bfloat16
HIGHEST = lax.Precision.HIGHEST

EPS = 1e-6
SSM_GROUP = 16
SSM_CHUNK = 16
TOPK_MAX = 256
Q_BLOCK = 128
KEY_TILE = 256
TOKEN_TILE = 512
ONES_ROWS = 8
INT_MIN = -(2 ** 31)
NEG_INIT = -1e30
VMEM_LIMIT = 56 * 1024 * 1024


def _rms(x, axis):
    return x * lax.rsqrt(jnp.mean(x * x, axis=axis, keepdims=True) + EPS)


def _mod_kernel(c_ref, w_ref, b_ref, o_ref):
    cv = c_ref[...]
    cond = cv * jax.nn.sigmoid(cv)
    o_ref[...] = jnp.dot(cond, w_ref[...], precision=HIGHEST, preferred_element_type=F32) + b_ref[...]


def _modulation(c, w_mod, b_mod):
    bsz, d = c.shape
    n = w_mod.shape[1]
    rows = -(-bsz // 8) * 8
    cp = jnp.pad(c, ((0, rows - bsz), (0, 0)))
    tn = d
    out = pl.pallas_call(
        _mod_kernel,
        grid=(n // tn,),
        in_specs=[pl.BlockSpec((rows, d), lambda j: (0, 0)),
                  pl.BlockSpec((d, tn), lambda j: (0, j)),
                  pl.BlockSpec((1, tn), lambda j: (0, j))],
        out_specs=pl.BlockSpec((rows, tn), lambda j: (0, j)),
        out_shape=jax.ShapeDtypeStruct((rows, n), F32),
        name="modulation",
    )(cp, w_mod, b_mod.reshape(1, n))
    return out[:bsz]


def _inproj_kernel(x_ref, sc_ref, sh_ref, g1_ref, wn_ref, wt_ref, gkv_ref, gkvc_ref, gki_ref,
                   xs_ref, ga_ref, gb_ref, ckv_ref, ki_ref, qT_ref, qiT_ref, ckvT_ref, wiT_ref,
                   *, n_xs, n_gate, n_kv, n_ki, n_q, n_qi, n_wi):
    x = x_ref[0]
    u = _rms(x, -1) * g1_ref[...]
    u = u * (1.0 + sc_ref[0]) + sh_ref[0]
    ub = u.astype(BF16)
    tm = ub.shape[0]

    zn = jnp.dot(ub, wn_ref[...], preferred_element_type=F32)
    o = 0
    xs_ref[0] = zn[:, o:o + n_xs].astype(BF16)
    o += n_xs
    ga_ref[0] = zn[:, o:o + n_gate].astype(BF16)
    o += n_gate
    gb_ref[0] = zn[:, o:o + n_gate].astype(BF16)
    o += n_gate
    ckv_ref[0] = (_rms(zn[:, o:o + n_kv], -1) * gkv_ref[...]).astype(BF16)
    o += n_kv
    ki_ref[0] = (_rms(zn[:, o:o + n_ki], -1) * gki_ref[...]).astype(BF16)

    zt = lax.dot_general(wt_ref[...], ub, (((1,), (1,)), ((), ())), preferred_element_type=F32)
    o = 0
    qT_ref[0] = zt[o:o + n_q].astype(BF16)
    o += n_q
    qiT_ref[0] = zt[o:o + n_qi].astype(BF16)
    o += n_qi
    ct = (_rms(zt[o:o + n_kv], 0) * gkvc_ref[...]).astype(BF16)
    o += n_kv
    wiT_ref[0] = zt[o:o + n_wi]
    ones_blk = jnp.where(lax.broadcasted_iota(jnp.int32, (ONES_ROWS, KEY_TILE), 0) == 0, 1.0, 0.0).astype(BF16)
    for i in range(tm // KEY_TILE):
        ckvT_ref[0, i, 0:n_kv, :] = ct[:, i * KEY_TILE:(i + 1) * KEY_TILE]
        ckvT_ref[0, i, n_kv:n_kv + ONES_ROWS, :] = ones_blk


def _in_projection(x, scale1, shift1, norm1_g, w_nat, w_tr, kv_g, ki_g, sizes):
    bsz, seq, d = x.shape
    n_xs, n_gate, n_kv, n_ki, n_q, n_qi, n_wi = sizes
    tm = TOKEN_TILE
    nt = seq // tm
    kern = functools.partial(_inproj_kernel, n_xs=n_xs, n_gate=n_gate, n_kv=n_kv, n_ki=n_ki,
                             n_q=n_q, n_qi=n_qi, n_wi=n_wi)
    tok = lambda w: pl.BlockSpec((1, tm, w), lambda b, t: (b, t, 0))
    lane = lambda r: pl.BlockSpec((1, r, tm), lambda b, t: (b, 0, t))
    vec = lambda w: pl.BlockSpec((1, 1, w), lambda b, t: (b, 0, 0))
    full = lambda a: pl.BlockSpec(a.shape, lambda b, t: (0,) * a.ndim)
    g1 = norm1_g.reshape(1, d)
    gkv = kv_g.reshape(1, n_kv)
    gkvc = kv_g.reshape(n_kv, 1)
    gki = ki_g.reshape(1, n_ki)
    out_shape = (
        jax.ShapeDtypeStruct((bsz, seq, n_xs), BF16),
        jax.ShapeDtypeStruct((bsz, seq, n_gate), BF16),
        jax.ShapeDtypeStruct((bsz, seq, n_gate), BF16),
        jax.ShapeDtypeStruct((bsz, seq, n_kv), BF16),
        jax.ShapeDtypeStruct((bsz, seq, n_ki), BF16),
        jax.ShapeDtypeStruct((bsz, n_q, seq), BF16),
        jax.ShapeDtypeStruct((bsz, n_qi, seq), BF16),
        jax.ShapeDtypeStruct((bsz, seq // KEY_TILE, n_kv + ONES_ROWS, KEY_TILE), BF16),
        jax.ShapeDtypeStruct((bsz, n_wi, seq), F32),
    )
    out_specs = (
        tok(n_xs), tok(n_gate), tok(n_gate), tok(n_kv), tok(n_ki),
        lane(n_q), lane(n_qi),
        pl.BlockSpec((1, tm // KEY_TILE, n_kv + ONES_ROWS, KEY_TILE), lambda b, t: (b, t, 0, 0)),
        lane(n_wi),
    )
    return pl.pallas_call(
        kern,
        grid=(bsz, nt),
        in_specs=[tok(d), vec(d), vec(d), full(g1), full(w_nat), full(w_tr), full(gkv), full(gkvc), full(gki)],
        out_specs=out_specs,
        out_shape=out_shape,
        compiler_params=pltpu.CompilerParams(vmem_limit_bytes=VMEM_LIMIT),
        name="in_projection",
    )(x, scale1, shift1, g1, w_nat, w_tr, gkv, gkvc, gki)


def _ssm_prep_kernel(are_ref, aim_ref, ldt_ref, bre_ref, bim_ref, cre_ref, cim_ref, ctre_ref, ctim_ref,
                     kall_ref, dre_ref, dim_ref, vre_ref, vimn_ref, lre_ref, lim_ref):
    ar = are_ref[0]
    ai = aim_ref[0]
    dt = jnp.exp(ldt_ref[0])
    ard = ar * dt
    aid = ai * dt

    def lam_pow(kf):
        mag = jnp.exp(ard * kf)
        ang = aid * kf
        return mag * jnp.cos(ang), mag * jnp.sin(ang)

    lb_re, lb_im = lam_pow(1.0)
    den = ar * ar + ai * ai
    coef_re = ((lb_re - 1.0) * ar + lb_im * ai) / den
    coef_im = (lb_im * ar - (lb_re - 1.0) * ai) / den
    bre = bre_ref[0]
    bim = bim_ref[0]
    bb_re = coef_re * bre - coef_im * bim
    bb_im = coef_re * bim + coef_im * bre

    width = bre.shape[1]
    lag = lax.shift_right_logical(lax.broadcasted_iota(jnp.int32, (1, width), 1), 4).astype(F32)
    lk_re, lk_im = lam_pow(lag)
    d_re = lk_re * bb_re - lk_im * bb_im
    d_im = lk_re * bb_im + lk_im * bb_re
    dre_ref[0] = d_re
    dim_ref[0] = d_im
    kall_ref[0] = (jnp.dot(cre_ref[0], d_re, precision=HIGHEST, preferred_element_type=F32)
                   - jnp.dot(cim_ref[0], d_im, precision=HIGHEST, preferred_element_type=F32))
    l1_re, l1_im = lam_pow(lag + 1.0)
    ctre = ctre_ref[0]
    ctim = ctim_ref[0]
    vre_ref[0] = l1_re * ctre - l1_im * ctim
    vimn_ref[0] = -(l1_re * ctim + l1_im * ctre)
    lt_re, lt_im = lam_pow(float(SSM_CHUNK))
    lre_ref[0] = lt_re
    lim_ref[0] = lt_im


def _ssm_prep(a_re, a_im, log_dt, b_re, b_im, c_re, c_im):
    g, p = a_re.shape
    cg = b_re.shape[2]
    width = SSM_CHUNK * cg
    col = lambda a: a.reshape(g, p, 1)
    b_t = lambda b: jnp.tile(b, (1, 1, SSM_CHUNK))
    c_t = lambda cc: jnp.tile(jnp.swapaxes(cc, 1, 2), (1, 1, SSM_CHUNK))
    blk = lambda s: pl.BlockSpec((1,) + s, lambda i: (i, 0, 0))
    outs = pl.pallas_call(
        _ssm_prep_kernel,
        grid=(g,),
        in_specs=[blk((p, 1)), blk((p, 1)), blk((1, 1)), blk((p, width)), blk((p, width)),
                  blk((cg, p)), blk((cg, p)), blk((p, width)), blk((p, width))],
        out_specs=(blk((cg, width)), blk((p, width)), blk((p, width)), blk((p, width)), blk((p, width)),
                   blk((p, 1)), blk((p, 1))),
        out_shape=(jax.ShapeDtypeStruct((g, cg, width), F32),) + (jax.ShapeDtypeStruct((g, p, width), F32),) * 4
        + (jax.ShapeDtypeStruct((g, p, 1), F32),) * 2,
        name="ssm_prep",
    )(col(a_re), col(a_im), log_dt.reshape(g, 1, 1), b_t(b_re), b_t(b_im), c_re, c_im, c_t(c_re), c_t(c_im))
    return outs


def _ssm_matrices(kall, d_re, d_im, v_re, v_imn, l_re, l_im, d_skip):
    g, cg, width = kall.shape
    p = d_re.shape[1]
    t = SSM_CHUNK
    npair = g // 2
    k4 = kall.reshape(g, cg, t, cg)
    jj = jnp.arange(t)[:, None]
    tt = jnp.arange(t)[None, :]
    lagidx = jnp.clip(tt - jj, 0, t - 1)
    m = k4[:, :, lagidx, :]
    m = jnp.where((tt >= jj)[None, None, :, :, None], m, 0.0)
    mt = jnp.transpose(m, (0, 2, 4, 3, 1)).reshape(g, width, width)
    mt = mt.reshape(npair, 2, width, width).astype(BF16)
    def w_of(d):
        d4 = d.reshape(g, p, t, cg)[:, :, ::-1, :]
        return jnp.transpose(d4, (0, 2, 3, 1)).reshape(g, width, p)
    wre = w_of(d_re).reshape(npair, 2, width, p)
    wim = w_of(d_im).reshape(npair, 2, width, p)
    z = jnp.zeros_like(wre[:, 0])
    mw = jnp.concatenate([
        jnp.concatenate([wre[:, 0], z, wim[:, 0], z], axis=2),
        jnp.concatenate([z, wre[:, 1], z, wim[:, 1]], axis=2)], axis=1).astype(BF16)
    vre = v_re.reshape(npair, 2, p, width)
    vim = v_imn.reshape(npair, 2, p, width)
    zv = jnp.zeros_like(vre[:, 0])
    vp = jnp.concatenate([
        jnp.concatenate([vre[:, 0], zv], axis=2),
        jnp.concatenate([zv, vre[:, 1]], axis=2),
        jnp.concatenate([vim[:, 0], zv], axis=2),
        jnp.concatenate([zv, vim[:, 1]], axis=2)], axis=1).astype(BF16)
    lam_re = l_re.reshape(1, g * p)
    lam_im = l_im.reshape(1, g * p)
    dsk = jnp.broadcast_to(d_skip.reshape(npair, 2, 1, cg), (npair, 2, t, cg)).reshape(npair, 1, 2 * width)
    return mt, mw, vp, lam_re, lam_im, dsk


def _ssm_u_kernel(z_ref, mw_ref, ure_ref, uim_ref):
    r = jnp.dot(z_ref[0, 0], mw_ref[0], preferred_element_type=F32)
    half = r.shape[1] // 2
    ure_ref[0] = r[:, :half]
    uim_ref[0] = r[:, half:]


def _ssm_scan_kernel(ure_ref, uim_ref, lre_ref, lim_ref, sre_ref, sim_ref):
    lr = lre_ref[...]
    li = lim_ref[...]
    nch = ure_ref.shape[1]

    def body(i, carry):
        s_re, s_im = carry
        sre_ref[0, pl.ds(i, 1), :] = s_re
        sim_ref[0, pl.ds(i, 1), :] = s_im
        u_re = ure_ref[0, pl.ds(i, 1), :]
        u_im = uim_ref[0, pl.ds(i, 1), :]
        return (lr * s_re - li * s_im + u_re, lr * s_im + li * s_re + u_im)

    zero = jnp.zeros_like(lr)
    lax.fori_loop(0, nch, body, (zero, zero))


def _gelu_tanh(y):
    return 0.5 * y * (1.0 + jnp.tanh(0.7978845608028654 * (y + 0.044715 * (y * y * y))))


def _ssm_y_kernel(z_ref, mt_ref, v_ref, sre_ref, sim_ref, d_ref, y_ref):
    z = z_ref[0, 0]
    half = z.shape[1] // 2
    ya = jnp.dot(z[:, :half], mt_ref[0, 0], preferred_element_type=F32)
    yb = jnp.dot(z[:, half:], mt_ref[0, 1], preferred_element_type=F32)
    s = jnp.concatenate([sre_ref[0], sim_ref[0]], axis=1).astype(BF16)
    y = jnp.concatenate([ya, yb], axis=1) + jnp.dot(s, v_ref[0], preferred_element_type=F32)
    y = y + d_ref[0] * z.astype(F32)
    y_ref[0, 0] = _gelu_tanh(y).astype(BF16)


def _s5_branch(xs, mats):
    mt, mw, vp, lam_re, lam_im, dsk = mats
    bsz, seq, width_all = xs.shape
    npair = mt.shape[0]
    cg = SSM_GROUP
    t = SSM_CHUNK
    nch = seq // t
    pw = 2 * t * cg
    sw = mw.shape[2] // 2
    z = xs.reshape(bsz, nch, t, npair, 2, cg)
    z = jnp.transpose(z, (0, 3, 1, 4, 2, 5)).reshape(bsz, npair, nch, pw)

    zspec = pl.BlockSpec((1, 1, nch, pw), lambda b, i: (b, i, 0, 0))
    sspec = pl.BlockSpec((1, nch, sw), lambda b, i: (b, 0, i))
    state = jax.ShapeDtypeStruct((bsz, nch, npair * sw), F32)
    u_re, u_im = pl.pallas_call(
        _ssm_u_kernel,
        grid=(bsz, npair),
        in_specs=[zspec, pl.BlockSpec((1,) + mw.shape[1:], lambda b, i: (i, 0, 0))],
        out_specs=(sspec, sspec),
        out_shape=(state, state),
        name="ssm_state_in",
    )(z, mw)

    full_state = pl.BlockSpec((1, nch, npair * sw), lambda b: (b, 0, 0))
    lam_spec = pl.BlockSpec(lam_re.shape, lambda b: (0, 0))
    s_re, s_im = pl.pallas_call(
        _ssm_scan_kernel,
        grid=(bsz,),
        in_specs=[full_state, full_state, lam_spec, lam_spec],
        out_specs=(full_state, full_state),
        out_shape=(state, state),
        compiler_params=pltpu.CompilerParams(vmem_limit_bytes=VMEM_LIMIT),
        name="ssm_scan",
    )(u_re, u_im, lam_re, lam_im)

    y = pl.pallas_call(
        _ssm_y_kernel,
        grid=(bsz, npair),
        in_specs=[zspec,
                  pl.BlockSpec((1,) + mt.shape[1:], lambda b, i: (i, 0, 0, 0)),
                  pl.BlockSpec((1,) + vp.shape[1:], lambda b, i: (i, 0, 0)),
                  sspec, sspec,
                  pl.BlockSpec((1, 1, pw), lambda b, i: (i, 0, 0))],
        out_specs=zspec,
        out_shape=jax.ShapeDtypeStruct((bsz, npair, nch, pw), BF16),
        name="ssm_out",
    )(z, mt, vp, s_re, s_im, dsk)
    y = y.reshape(bsz, npair, nch, 2, t, cg)
    return jnp.transpose(y, (0, 2, 4, 1, 3, 5)).reshape(bsz, seq, width_all)


def _dsa_kernel(qT_ref, qiT_ref, wiT_ref, ki_ref, ckv_ref, ckvT_ref, wuk_ref, wuvT_ref, wproj_ref,
                o_ref, keys_ref, bias_ref, qlT_ref, acc_ref, m_ref,
                *, topk, n_heads, d_head, n_iheads, d_ihead, n_kv):
    qb = Q_BLOCK
    tk = KEY_TILE
    j = pl.program_id(1)
    n_tiles = (j * qb + qb + tk - 1) // tk

    for h in range(n_heads):
        qh = qT_ref[0, h * d_head:(h + 1) * d_head, :]
        ql = jnp.dot(wuk_ref[h], qh, preferred_element_type=F32) * (d_head ** -0.5)
        qlT_ref[:, h * qb:(h + 1) * qb] = ql.astype(BF16)
    qi_cat = jnp.concatenate([qiT_ref[0, h * d_ihead:(h + 1) * d_ihead, :] for h in range(n_iheads)], axis=1)
    w_heads = wiT_ref[0] * (n_iheads ** -0.5 * d_ihead ** -0.5)

    q_pos = j * qb + lax.broadcasted_iota(jnp.int32, (tk, qb), 1)
    row_iota = lax.broadcasted_iota(jnp.int32, (tk, qb), 0)

    def score_body(kt, carry):
        k0 = pl.multiple_of(kt * tk, tk)
        rel = jnp.dot(ki_ref[0, pl.ds(k0, tk), :], qi_cat, preferred_element_type=F32)
        sc = None
        for h in range(n_iheads):
            term = jnp.maximum(rel[:, h * qb:(h + 1) * qb], 0.0) * w_heads[h:h + 1, :]
            sc = term if sc is None else sc + term
        bits = pltpu.bitcast(sc, jnp.int32)
        key = bits ^ (lax.shift_right_arithmetic(bits, 31) & jnp.int32(0x7FFFFFFF))
        keys_ref[kt] = jnp.where((k0 + row_iota) <= q_pos, key, jnp.int32(INT_MIN))
        return carry

    lax.fori_loop(0, n_tiles, score_body, 0)

    def bit_body(i, t_u):
        cand_u = t_u | lax.shift_left(jnp.int32(1), 31 - i)
        cand_s = cand_u ^ jnp.int32(INT_MIN)

        def cnt_body(kt, acc):
            return acc + jnp.sum((keys_ref[kt] >= cand_s).astype(jnp.int32), axis=0, keepdims=True)

        cnt = lax.fori_loop(0, n_tiles, cnt_body, jnp.zeros((1, qb), jnp.int32))
        return jnp.where(cnt >= topk, cand_u, t_u)

    t_u = lax.fori_loop(0, 32, bit_body, jnp.zeros((1, qb), jnp.int32))
    thr = jnp.maximum(t_u ^ jnp.int32(INT_MIN), jnp.int32(INT_MIN + 1))

    def gt_body(kt, acc):
        return acc + jnp.sum((keys_ref[kt] > thr).astype(jnp.int32), axis=0, keepdims=True)

    n_gt = lax.fori_loop(0, n_tiles, gt_body, jnp.zeros((1, qb), jnp.int32))
    need = (topk - n_gt).astype(F32)

    lower = (lax.broadcasted_iota(jnp.int32, (tk, tk), 0) >= lax.broadcasted_iota(jnp.int32, (tk, tk), 1))
    lower = jnp.where(lower, 1.0, 0.0).astype(BF16)

    def bias_body(kt, seen):
        kk = keys_ref[kt]
        eq = kk == thr
        eqf = jnp.where(eq, 1.0, 0.0).astype(BF16)
        rank = jnp.dot(lower, eqf, preferred_element_type=F32) + seen
        keep_tie = jnp.where(rank <= need, 0.0, -jnp.inf)
        bias_ref[kt] = jnp.where(kk > thr, 0.0, jnp.where(eq, keep_tie, -jnp.inf))
        return rank[tk - 1:tk, :]

    lax.fori_loop(0, n_tiles, bias_body, jnp.zeros((1, qb), F32))

    m_ref[...] = jnp.full(m_ref.shape, NEG_INIT, F32)
    acc_ref[...] = jnp.zeros(acc_ref.shape, F32)

    def att_body(kt, carry):
        k0 = pl.multiple_of(kt * tk, tk)
        s = jnp.dot(ckv_ref[0, pl.ds(k0, tk), :], qlT_ref[...], preferred_element_type=F32)
        s = s + jnp.concatenate([bias_ref[kt]] * n_heads, axis=1)
        m_old = m_ref[...]
        m_new = jnp.maximum(m_old, jnp.max(s, axis=0, keepdims=True))
        p = jnp.exp(s - m_new).astype(BF16)
        acc_ref[...] = acc_ref[...] * jnp.exp(m_old - m_new) + jnp.dot(
            ckvT_ref[0, kt], p, preferred_element_type=F32)
        m_ref[...] = m_new
        return carry

    lax.fori_loop(0, n_tiles, att_body, 0)

    inv_l = 1.0 / acc_ref[n_kv:n_kv + 1, :]
    parts = []
    for h in range(n_heads):
        o_h = (acc_ref[0:n_kv, h * qb:(h + 1) * qb] * inv_l[:, h * qb:(h + 1) * qb]).astype(BF16)
        parts.append(jnp.dot(wuvT_ref[h], o_h, preferred_element_type=F32))
    t_tok = jnp.concatenate(parts, axis=0).T.astype(BF16)
    o_ref[0] = jnp.dot(t_tok, wproj_ref[...], preferred_element_type=F32).astype(BF16)


def _dsa_branch(qT, qiT, wiT, ki, ckv, ckvT, w_uk, w_uv, w_proj, topk):
    bsz, n_q, seq = qT.shape
    n_kv, n_heads, d_head = w_uk.shape
    n_ki = ki.shape[2]
    n_iheads = qiT.shape[1] // n_ki
    d_out = w_proj.shape[1]
    qb, tk = Q_BLOCK, KEY_TILE
    wuk = jnp.transpose(w_uk, (1, 0, 2)).astype(BF16)
    wuvT = jnp.transpose(w_uv, (1, 2, 0)).astype(BF16)
    wproj = w_proj.astype(BF16)
    kern = functools.partial(_dsa_kernel, topk=topk, n_heads=n_heads, d_head=d_head,
                             n_iheads=n_iheads, d_ihead=n_ki, n_kv=n_kv)
    lane = lambda r: pl.BlockSpec((1, r, qb), lambda b, j: (b, 0, j))
    per_b = lambda a: pl.BlockSpec((1,) + a.shape[1:], lambda b, j: (b,) + (0,) * (a.ndim - 1))
    full = lambda a: pl.BlockSpec(a.shape, lambda b, j: (0,) * a.ndim)
    return pl.pallas_call(
        kern,
        grid=(bsz, seq // qb),
        in_specs=[lane(n_q), lane(qiT.shape[1]), lane(wiT.shape[1]), per_b(ki), per_b(ckv), per_b(ckvT),
                  full(wuk), full(wuvT), full(wproj)],
        out_specs=pl.BlockSpec((1, qb, d_out), lambda b, j: (b, j, 0)),
        out_shape=jax.ShapeDtypeStruct((bsz, seq, d_out), BF16),
        scratch_shapes=[pltpu.VMEM((seq // tk, tk, qb), jnp.int32),
                        pltpu.VMEM((seq // tk, tk, qb), F32),
                        pltpu.VMEM((n_kv, n_heads * qb), BF16),
                        pltpu.VMEM((n_kv + ONES_ROWS, n_heads * qb), F32),
                        pltpu.VMEM((1, n_heads * qb), F32)],
        compiler_params=pltpu.CompilerParams(vmem_limit_bytes=VMEM_LIMIT),
        name="dsa",
    )(qT, qiT, wiT, ki, ckv, ckvT, wuk, wuvT, wproj)


def _ffn_kernel(x_ref, y_ref, a_ref, ga_ref, gb_ref, g1_ref, sh2_ref, sc2_ref, g2_ref,
                wglu_ref, bglu_ref, wout_ref, n2_ref, wg_ref, wu_ref, wd_ref, fg_ref, o_ref, *, n_chunks):
    x = x_ref[0]
    z = jnp.dot(y_ref[0], wglu_ref[...], preferred_element_type=F32) + bglu_ref[...]
    d = z.shape[1] // 2
    y_ssm = z[:, :d] * jax.nn.sigmoid(z[:, d:])
    merged = (jax.nn.sigmoid(ga_ref[0].astype(F32)) * y_ssm
              + jax.nn.sigmoid(gb_ref[0].astype(F32)) * a_ref[0].astype(F32))
    h1 = x + g1_ref[0] * jnp.dot(merged.astype(BF16), wout_ref[...], preferred_element_type=F32)
    u2 = _rms(h1, -1) * n2_ref[...]
    u2 = (u2 * (1.0 + sc2_ref[0]) + sh2_ref[0]).astype(BF16)
    hidden = wg_ref.shape[1]
    step = hidden // n_chunks
    ffn = None
    for ci in range(n_chunks):
        sl = slice(ci * step, (ci + 1) * step)
        gate = jnp.dot(u2, wg_ref[:, sl], preferred_element_type=F32)
        up = jnp.dot(u2, wu_ref[:, sl], preferred_element_type=F32)
        act = ((gate * jax.nn.sigmoid(gate)) * up).astype(BF16)
        part = jnp.dot(act, wd_ref[sl, :], preferred_element_type=F32)
        ffn = part if ffn is None else ffn + part
    h2 = h1 + g2_ref[0] * ffn
    o_ref[0] = _rms(h2, -1) * fg_ref[...]


def _merge_ffn(x, y_gelu, y_att, ga, gb, gate1, shift2, scale2, gate2,
               w_glu, b_glu, w_out, norm2_g, w_gate, w_up, w_down, final_g):
    bsz, seq, d = x.shape
    tm = TOKEN_TILE
    hidden = w_gate.shape[1]
    n_chunks = 2 if hidden % 256 == 0 else 1
    tok = lambda w: pl.BlockSpec((1, tm, w), lambda b, t: (b, t, 0))
    vec = lambda w: pl.BlockSpec((1, 1, w), lambda b, t: (b, 0, 0))
    const = lambda a: pl.BlockSpec(a.shape, lambda b, t: (0,) * a.ndim, pipeline_mode=pl.Buffered(1))
    wglu = w_glu.astype(BF16)
    bglu = b_glu.reshape(1, -1)
    wout = w_out.astype(BF16)
    n2 = norm2_g.reshape(1, d)
    wg = w_gate.astype(BF16)
    wu = w_up.astype(BF16)
    wd = w_down.astype(BF16)
    fg = final_g.reshape(1, d)
    return pl.pallas_call(
        functools.partial(_ffn_kernel, n_chunks=n_chunks),
        grid=(bsz, seq // tm),
        in_specs=[tok(d), tok(y_gelu.shape[2]), tok(d), tok(d), tok(d), vec(d), vec(d), vec(d), vec(d),
                  const(wglu), const(bglu), const(wout), const(n2), const(wg), const(wu), const(wd), const(fg)],
        out_specs=tok(d),
        out_shape=jax.ShapeDtypeStruct((bsz, seq, d), F32),
        compiler_params=pltpu.CompilerParams(vmem_limit_bytes=VMEM_LIMIT),
        name="merge_ffn",
    )(x, y_gelu, y_att, ga, gb, gate1, shift2, scale2, gate2, wglu, bglu, wout, n2, wg, wu, wd, fg)


def kernel(x, c, w_mod, b_mod, norm1_g, w_in, ssm_a_re, ssm_a_im, ssm_log_dt, ssm_b_re, ssm_b_im, ssm_c_re, ssm_c_im, ssm_d, w_ssm_glu, b_ssm_glu, kv_norm_g, idx_k_norm_g, w_uk, w_uv, w_attn_proj, w_out, norm2_g, w_ffn_gate, w_ffn_up, w_ffn_down, final_g):
    bsz, seq, d = x.shape
    assert w_mod.shape[0] == 1, "single-layer block"
    assert seq % TOKEN_TILE == 0 and seq % KEY_TILE == 0 and TOKEN_TILE % KEY_TILE == 0
    g, p = ssm_a_re.shape[1:]
    cg = ssm_b_re.shape[3]
    assert cg == SSM_GROUP and g % 2 == 0
    n_xs = g * cg
    n_kv, n_heads, d_head = w_uk.shape[1:]
    n_q = n_heads * d_head
    n_ki = idx_k_norm_g.shape[1]
    n_wi = w_in.shape[2] - (n_xs + n_q + n_kv + n_ki + 2 * d)
    n_wi = n_wi // (n_ki + 1)
    n_qi = n_wi * n_ki
    topk = min(TOPK_MAX, seq // 4)

    mod = _modulation(c, w_mod[0], b_mod[0])
    shift1, scale1, gate1, shift2, scale2, gate2 = [m.reshape(bsz, 1, d) for m in jnp.split(mod, 6, axis=-1)]

    offs = {}
    o = 0
    for name, size in (("xs", n_xs), ("q", n_q), ("ckv", n_kv), ("qi", n_qi), ("ki", n_ki), ("wi", n_wi),
                       ("ga", d), ("gb", d)):
        offs[name] = (o, o + size)
        o += size
    assert o == w_in.shape[2]
    wi_all = w_in[0]
    cols = lambda n: wi_all[:, offs[n][0]:offs[n][1]]
    wi_rows = -(-n_wi // 8) * 8
    w_nat = jnp.concatenate([cols("xs"), cols("ga"), cols("gb"), cols("ckv"), cols("ki")], axis=1).astype(BF16)
    w_tr = jnp.concatenate([cols("q"), cols("qi"), cols("ckv"),
                            jnp.pad(cols("wi"), ((0, 0), (0, wi_rows - n_wi)))], axis=1).T.astype(BF16)
    xs, ga, gb, ckv, ki, qT, qiT, ckvT, wiT = _in_projection(
        x, scale1, shift1, norm1_g[0], w_nat, w_tr, kv_norm_g[0], idx_k_norm_g[0],
        (n_xs, d, n_kv, n_ki, n_q, n_qi, wi_rows))

    prep = _ssm_prep(ssm_a_re[0], ssm_a_im[0], ssm_log_dt[0], ssm_b_re[0], ssm_b_im[0], ssm_c_re[0], ssm_c_im[0])
    y_gelu = _s5_branch(xs, _ssm_matrices(*prep, ssm_d[0]))

    y_att = _dsa_branch(qT, qiT, wiT, ki, ckv, ckvT, w_uk[0], w_uv[0], w_attn_proj[0], topk)

    return _merge_ffn(x, y_gelu, y_att, ga, gb, gate1, shift2, scale2, gate2,
                      w_ssm_glu[0], b_ssm_glu[0], w_out[0], norm2_g[0],
                      w_ffn_gate[0], w_ffn_up[0], w_ffn_down[0], final_g)
```

```python
import functools

import jax
import jax.numpy as jnp
from jax import lax
from jax.experimental import pallas as pl
from jax.experimental.pallas import tpu as pltpu

F32 = jnp.float32
BF16 = jnp.bfloat16
HIGHEST = lax.Precision.HIGHEST

EPS = 1e-6
SSM_GROUP = 16
SSM_CHUNK = 16
TOPK_MAX = 256
Q_BLOCK = 128
KEY_TILE = 256
TOKEN_TILE = 512
ONES_ROWS = 8
INT_MIN = -(2 ** 31)
NEG_INIT = -1e30
LOG2E = 1.4426950408889634
VMEM_LIMIT = 56 * 1024 * 1024


def _rms(x, axis):
    return x * lax.rsqrt(jnp.mean(x * x, axis=axis, keepdims=True) + EPS)


def _mod_kernel(c_ref, w_ref, b_ref, o_ref):
    cv = c_ref[...]
    cond = cv * jax.nn.sigmoid(cv)
    o_ref[...] = jnp.dot(cond, w_ref[...], precision=HIGHEST, preferred_element_type=F32) + b_ref[...]


def _modulation(c, w_mod, b_mod):
    bsz, d = c.shape
    n = w_mod.shape[1]
    rows = -(-bsz // 8) * 8
    cp = jnp.pad(c, ((0, rows - bsz), (0, 0)))
    tn = d
    out = pl.pallas_call(
        _mod_kernel,
        grid=(n // tn,),
        in_specs=[pl.BlockSpec((rows, d), lambda j: (0, 0)),
                  pl.BlockSpec((d, tn), lambda j: (0, j)),
                  pl.BlockSpec((1, tn), lambda j: (0, j))],
        out_specs=pl.BlockSpec((rows, tn), lambda j: (0, j)),
        out_shape=jax.ShapeDtypeStruct((rows, n), F32),
        name="modulation",
    )(cp, w_mod, b_mod.reshape(1, n))
    return out[:bsz]


def _inproj_kernel(x_ref, sc_ref, sh_ref, g1_ref, wn_ref, wt_ref, gkv_ref, gkvc_ref, gki_ref,
                   xs_ref, ga_ref, gb_ref, ckv_ref, ki_ref, qT_ref, qiT_ref, ckvT_ref, wiT_ref,
                   *, n_xs, n_gate, n_kv, n_ki, n_q, n_qi, n_wi):
    x = x_ref[0]
    u = _rms(x, -1) * g1_ref[...]
    u = u * (1.0 + sc_ref[0]) + sh_ref[0]
    ub = u.astype(BF16)
    tm = ub.shape[0]

    zn = jnp.dot(ub, wn_ref[...], preferred_element_type=F32)
    o = 0
    xs_ref[0] = zn[:, o:o + n_xs].astype(BF16)
    o += n_xs
    ga_ref[0] = zn[:, o:o + n_gate].astype(BF16)
    o += n_gate
    gb_ref[0] = zn[:, o:o + n_gate].astype(BF16)
    o += n_gate
    ckv_ref[0] = (_rms(zn[:, o:o + n_kv], -1) * gkv_ref[...]).astype(BF16)
    o += n_kv
    ki_ref[0] = (_rms(zn[:, o:o + n_ki], -1) * gki_ref[...]).astype(BF16)

    zt = lax.dot_general(wt_ref[...], ub, (((1,), (1,)), ((), ())), preferred_element_type=F32)
    o = 0
    qT_ref[0] = zt[o:o + n_q].astype(BF16)
    o += n_q
    qiT_ref[0] = zt[o:o + n_qi].astype(BF16)
    o += n_qi
    ct = (_rms(zt[o:o + n_kv], 0) * gkvc_ref[...]).astype(BF16)
    o += n_kv
    wiT_ref[0] = zt[o:o + n_wi]
    ones_blk = jnp.where(lax.broadcasted_iota(jnp.int32, (ONES_ROWS, KEY_TILE), 0) == 0, 1.0, 0.0).astype(BF16)
    for i in range(tm // KEY_TILE):
        ckvT_ref[0, i, 0:n_kv, :] = ct[:, i * KEY_TILE:(i + 1) * KEY_TILE]
        ckvT_ref[0, i, n_kv:n_kv + ONES_ROWS, :] = ones_blk


def _in_projection(x, scale1, shift1, norm1_g, w_nat, w_tr, kv_g, ki_g, sizes):
    bsz, seq, d = x.shape
    n_xs, n_gate, n_kv, n_ki, n_q, n_qi, n_wi = sizes
    tm = TOKEN_TILE
    nt = seq // tm
    kern = functools.partial(_inproj_kernel, n_xs=n_xs, n_gate=n_gate, n_kv=n_kv, n_ki=n_ki,
                             n_q=n_q, n_qi=n_qi, n_wi=n_wi)
    tok = lambda w: pl.BlockSpec((1, tm, w), lambda b, t: (b, t, 0))
    lane = lambda r: pl.BlockSpec((1, r, tm), lambda b, t: (b, 0, t))
    vec = lambda w: pl.BlockSpec((1, 1, w), lambda b, t: (b, 0, 0))
    full = lambda a: pl.BlockSpec(a.shape, lambda b, t: (0,) * a.ndim)
    g1 = norm1_g.reshape(1, d)
    gkv = kv_g.reshape(1, n_kv)
    gkvc = kv_g.reshape(n_kv, 1)
    gki = ki_g.reshape(1, n_ki)
    out_shape = (
        jax.ShapeDtypeStruct((bsz, seq, n_xs), BF16),
        jax.ShapeDtypeStruct((bsz, seq, n_gate), BF16),
        jax.ShapeDtypeStruct((bsz, seq, n_gate), BF16),
        jax.ShapeDtypeStruct((bsz, seq, n_kv), BF16),
        jax.ShapeDtypeStruct((bsz, seq, n_ki), BF16),
        jax.ShapeDtypeStruct((bsz, n_q, seq), BF16),
        jax.ShapeDtypeStruct((bsz, n_qi, seq), BF16),
        jax.ShapeDtypeStruct((bsz, seq // KEY_TILE, n_kv + ONES_ROWS, KEY_TILE), BF16),
        jax.ShapeDtypeStruct((bsz, n_wi, seq), F32),
    )
    out_specs = (
        tok(n_xs), tok(n_gate), tok(n_gate), tok(n_kv), tok(n_ki),
        lane(n_q), lane(n_qi),
        pl.BlockSpec((1, tm // KEY_TILE, n_kv + ONES_ROWS, KEY_TILE), lambda b, t: (b, t, 0, 0)),
        lane(n_wi),
    )
    return pl.pallas_call(
        kern,
        grid=(bsz, nt),
        in_specs=[tok(d), vec(d), vec(d), full(g1), full(w_nat), full(w_tr), full(gkv), full(gkvc), full(gki)],
        out_specs=out_specs,
        out_shape=out_shape,
        compiler_params=pltpu.CompilerParams(vmem_limit_bytes=VMEM_LIMIT),
        name="in_projection",
    )(x, scale1, shift1, g1, w_nat, w_tr, gkv, gkvc, gki)


def _ssm_prep_kernel(are_ref, aim_ref, ldt_ref, bre_ref, bim_ref, cre_ref, cim_ref, ctre_ref, ctim_ref,
                     kall_ref, dre_ref, dim_ref, vre_ref, vimn_ref, lre_ref, lim_ref):
    ar = are_ref[0]
    ai = aim_ref[0]
    dt = jnp.exp(ldt_ref[0])
    ard = ar * dt
    aid = ai * dt

    def lam_pow(kf):
        mag = jnp.exp(ard * kf)
        ang = aid * kf
        return mag * jnp.cos(ang), mag * jnp.sin(ang)

    lb_re, lb_im = lam_pow(1.0)
    den = ar * ar + ai * ai
    coef_re = ((lb_re - 1.0) * ar + lb_im * ai) / den
    coef_im = (lb_im * ar - (lb_re - 1.0) * ai) / den
    bre = bre_ref[0]
    bim = bim_ref[0]
    bb_re = coef_re * bre - coef_im * bim
    bb_im = coef_re * bim + coef_im * bre

    width = bre.shape[1]
    lag = lax.shift_right_logical(lax.broadcasted_iota(jnp.int32, (1, width), 1), 4).astype(F32)
    lk_re, lk_im = lam_pow(lag)
    d_re = lk_re * bb_re - lk_im * bb_im
    d_im = lk_re * bb_im + lk_im * bb_re
    dre_ref[0] = d_re
    dim_ref[0] = d_im
    kall_ref[0] = (jnp.dot(cre_ref[0], d_re, precision=HIGHEST, preferred_element_type=F32)
                   - jnp.dot(cim_ref[0], d_im, precision=HIGHEST, preferred_element_type=F32))
    l1_re, l1_im = lam_pow(lag + 1.0)
    ctre = ctre_ref[0]
    ctim = ctim_ref[0]
    vre_ref[0] = l1_re * ctre - l1_im * ctim
    vimn_ref[0] = -(l1_re * ctim + l1_im * ctre)
    lt_re, lt_im = lam_pow(float(SSM_CHUNK))
    lre_ref[0] = lt_re
    lim_ref[0] = lt_im


def _ssm_prep(a_re, a_im, log_dt, b_re, b_im, c_re, c_im):
    g, p = a_re.shape
    cg = b_re.shape[2]
    width = SSM_CHUNK * cg
    col = lambda a: a.reshape(g, p, 1)
    b_t = lambda b: jnp.tile(b, (1, 1, SSM_CHUNK))
    c_t = lambda cc: jnp.tile(jnp.swapaxes(cc, 1, 2), (1, 1, SSM_CHUNK))
    blk = lambda s: pl.BlockSpec((1,) + s, lambda i: (i, 0, 0))
    outs = pl.pallas_call(
        _ssm_prep_kernel,
        grid=(g,),
        in_specs=[blk((p, 1)), blk((p, 1)), blk((1, 1)), blk((p, width)), blk((p, width)),
                  blk((cg, p)), blk((cg, p)), blk((p, width)), blk((p, width))],
        out_specs=(blk((cg, width)), blk((p, width)), blk((p, width)), blk((p, width)), blk((p, width)),
                   blk((p, 1)), blk((p, 1))),
        out_shape=(jax.ShapeDtypeStruct((g, cg, width), F32),) + (jax.ShapeDtypeStruct((g, p, width), F32),) * 4
        + (jax.ShapeDtypeStruct((g, p, 1), F32),) * 2,
        name="ssm_prep",
    )(col(a_re), col(a_im), log_dt.reshape(g, 1, 1), b_t(b_re), b_t(b_im), c_re, c_im, c_t(c_re), c_t(c_im))
    return outs


def _ssm_matrices(kall, d_re, d_im, v_re, v_imn, l_re, l_im, d_skip):
    g, cg, width = kall.shape
    p = d_re.shape[1]
    t = SSM_CHUNK
    npair = g // 2
    k4 = kall.reshape(g, cg, t, cg)
    jj = jnp.arange(t)[:, None]
    tt = jnp.arange(t)[None, :]
    lagidx = jnp.clip(tt - jj, 0, t - 1)
    m = k4[:, :, lagidx, :]
    m = jnp.where((tt >= jj)[None, None, :, :, None], m, 0.0)
    mt = jnp.transpose(m, (0, 2, 4, 3, 1)).reshape(g, width, width)
    mt = mt.reshape(npair, 2, width, width).astype(BF16)
    def w_of(d):
        d4 = d.reshape(g, p, t, cg)[:, :, ::-1, :]
        return jnp.transpose(d4, (0, 2, 3, 1)).reshape(g, width, p)
    wre = w_of(d_re).reshape(npair, 2, width, p)
    wim = w_of(d_im).reshape(npair, 2, width, p)
    z = jnp.zeros_like(wre[:, 0])
    mw = jnp.concatenate([
        jnp.concatenate([wre[:, 0], z, wim[:, 0], z], axis=2),
        jnp.concatenate([z, wre[:, 1], z, wim[:, 1]], axis=2)], axis=1).astype(BF16)
    vre = v_re.reshape(npair, 2, p, width)
    vim = v_imn.reshape(npair, 2, p, width)
    zv = jnp.zeros_like(vre[:, 0])
    vp = jnp.concatenate([
        jnp.concatenate([vre[:, 0], zv], axis=2),
        jnp.concatenate([zv, vre[:, 1]], axis=2),
        jnp.concatenate([vim[:, 0], zv], axis=2),
        jnp.concatenate([zv, vim[:, 1]], axis=2)], axis=1).astype(BF16)
    lam_re = l_re.reshape(1, g * p)
    lam_im = l_im.reshape(1, g * p)
    dsk = jnp.broadcast_to(d_skip.reshape(npair, 2, 1, cg), (npair, 2, t, cg)).reshape(npair, 1, 2 * width)
    return mt, mw, vp, lam_re, lam_im, dsk


def _ssm_u_kernel(z_ref, mw_ref, ure_ref, uim_ref):
    r = jnp.dot(z_ref[0, 0], mw_ref[0], preferred_element_type=F32)
    half = r.shape[1] // 2
    ure_ref[0] = r[:, :half]
    uim_ref[0] = r[:, half:]


def _ssm_scan_kernel(ure_ref, uim_ref, lre_ref, lim_ref, sre_ref, sim_ref):
    lr = lre_ref[...]
    li = lim_ref[...]
    nch = ure_ref.shape[1]

    def body(i, carry):
        s_re, s_im = carry
        sre_ref[0, pl.ds(i, 1), :] = s_re
        sim_ref[0, pl.ds(i, 1), :] = s_im
        u_re = ure_ref[0, pl.ds(i, 1), :]
        u_im = uim_ref[0, pl.ds(i, 1), :]
        return (lr * s_re - li * s_im + u_re, lr * s_im + li * s_re + u_im)

    zero = jnp.zeros_like(lr)
    lax.fori_loop(0, nch, body, (zero, zero))


def _gelu_tanh(y):
    return 0.5 * y * (1.0 + jnp.tanh(0.7978845608028654 * (y + 0.044715 * (y * y * y))))


def _ssm_y_kernel(z_ref, mt_ref, v_ref, sre_ref, sim_ref, d_ref, y_ref):
    z = z_ref[0, 0]
    half = z.shape[1] // 2
    ya = jnp.dot(z[:, :half], mt_ref[0, 0], preferred_element_type=F32)
    yb = jnp.dot(z[:, half:], mt_ref[0, 1], preferred_element_type=F32)
    s = jnp.concatenate([sre_ref[0], sim_ref[0]], axis=1).astype(BF16)
    y = jnp.concatenate([ya, yb], axis=1) + jnp.dot(s, v_ref[0], preferred_element_type=F32)
    y = y + d_ref[0] * z.astype(F32)
    y_ref[0, 0] = _gelu_tanh(y).astype(BF16)


def _s5_branch(xs, mats):
    mt, mw, vp, lam_re, lam_im, dsk = mats
    bsz, seq, width_all = xs.shape
    npair = mt.shape[0]
    cg = SSM_GROUP
    t = SSM_CHUNK
    nch = seq // t
    pw = 2 * t * cg
    sw = mw.shape[2] // 2
    z = xs.reshape(bsz, nch, t, npair, 2, cg)
    z = jnp.transpose(z, (0, 3, 1, 4, 2, 5)).reshape(bsz, npair, nch, pw)

    zspec = pl.BlockSpec((1, 1, nch, pw), lambda b, i: (b, i, 0, 0))
    sspec = pl.BlockSpec((1, nch, sw), lambda b, i: (b, 0, i))
    state = jax.ShapeDtypeStruct((bsz, nch, npair * sw), F32)
    u_re, u_im = pl.pallas_call(
        _ssm_u_kernel,
        grid=(bsz, npair),
        in_specs=[zspec, pl.BlockSpec((1,) + mw.shape[1:], lambda b, i: (i, 0, 0))],
        out_specs=(sspec, sspec),
        out_shape=(state, state),
        name="ssm_state_in",
    )(z, mw)

    full_state = pl.BlockSpec((1, nch, npair * sw), lambda b: (b, 0, 0))
    lam_spec = pl.BlockSpec(lam_re.shape, lambda b: (0, 0))
    s_re, s_im = pl.pallas_call(
        _ssm_scan_kernel,
        grid=(bsz,),
        in_specs=[full_state, full_state, lam_spec, lam_spec],
        out_specs=(full_state, full_state),
        out_shape=(state, state),
        compiler_params=pltpu.CompilerParams(vmem_limit_bytes=VMEM_LIMIT),
        name="ssm_scan",
    )(u_re, u_im, lam_re, lam_im)

    y = pl.pallas_call(
        _ssm_y_kernel,
        grid=(bsz, npair),
        in_specs=[zspec,
                  pl.BlockSpec((1,) + mt.shape[1:], lambda b, i: (i, 0, 0, 0)),
                  pl.BlockSpec((1,) + vp.shape[1:], lambda b, i: (i, 0, 0)),
                  sspec, sspec,
                  pl.BlockSpec((1, 1, pw), lambda b, i: (i, 0, 0))],
        out_specs=zspec,
        out_shape=jax.ShapeDtypeStruct((bsz, npair, nch, pw), BF16),
        name="ssm_out",
    )(z, mt, vp, s_re, s_im, dsk)
    y = y.reshape(bsz, npair, nch, 2, t, cg)
    return jnp.transpose(y, (0, 2, 4, 1, 3, 5)).reshape(bsz, seq, width_all)


def _i32(v):
    return jnp.int32(v - (1 << 32) if v >= (1 << 31) else v)


def _order_pattern(x):
    bits = pltpu.bitcast(x, jnp.int32)
    return bits ^ (lax.shift_right_arithmetic(bits, 31) | jnp.int32(INT_MIN))


def _pattern_to_float(u):
    bits = jnp.where(u < 0, u & jnp.int32(0x7FFFFFFF), ~u)
    return pltpu.bitcast(bits, F32)


def _finite_threshold(u):
    return _pattern_to_float(jnp.where(u < 0, u, jnp.maximum(u, jnp.int32(0x00800000))))


def _bit_transpose32(words):
    a = list(words)
    j, m = 16, 0x0000FFFF
    while j:
        k = 0
        while k < 32:
            t = (a[k] ^ lax.shift_right_logical(a[k + j], j)) & _i32(m)
            a[k] = a[k] ^ t
            a[k + j] = a[k + j] ^ lax.shift_left(t, j)
            k = (k + j + 1) & ~j
        j >>= 1
        m = (m ^ (m << j)) & 0xFFFFFFFF
    return a


def _dsa_kernel(qT_ref, qiT_ref, wiT_ref, ki_ref, ckv_ref, ckvT_ref, wuk_ref, wuvT_ref, wproj_ref,
                o_ref, scores_ref, planes_ref, cand_ref, sel_ref, bias_ref, qlT_ref, acc_ref, m_ref,
                sa_ref, ma_ref, sb_ref, mb_ref,
                *, topk, n_heads, d_head, n_iheads, d_ihead, n_kv):
    qb = Q_BLOCK
    tk = KEY_TILE
    words = tk // 8
    j = pl.program_id(1)
    n_tiles = (j * qb + qb + tk - 1) // tk

    for h in range(n_heads):
        qh = qT_ref[0, h * d_head:(h + 1) * d_head, :]
        ql = jnp.dot(wuk_ref[h], qh, preferred_element_type=F32) * (d_head ** -0.5 * LOG2E)
        qlT_ref[:, h * qb:(h + 1) * qb] = ql.astype(BF16)
    qi_cat = jnp.concatenate([qiT_ref[0, h * d_ihead:(h + 1) * d_ihead, :] for h in range(n_iheads)], axis=1)
    w_heads = wiT_ref[0] * (n_iheads ** -0.5 * d_ihead ** -0.5)

    q_pos = j * qb + lax.broadcasted_iota(jnp.int32, (tk, qb), 1)
    row_iota = lax.broadcasted_iota(jnp.int32, (tk, qb), 0)

    def score_body(kt, carry):
        k0 = pl.multiple_of(kt * tk, tk)
        rel = jnp.dot(ki_ref[0, pl.ds(k0, tk), :], qi_cat, preferred_element_type=F32)
        sc = None
        for h in range(n_iheads):
            term = jnp.maximum(rel[:, h * qb:(h + 1) * qb], 0.0) * w_heads[h:h + 1, :]
            sc = term if sc is None else sc + term
        sc = jnp.where((k0 + row_iota) <= q_pos, sc, -jnp.inf)
        scores_ref[kt] = sc
        pat = _order_pattern(sc)
        planes = _bit_transpose32([pat[8 * v:8 * v + 8] for v in range(words)])
        r0 = pl.multiple_of(kt * 8, 8)
        for p in range(32):
            planes_ref[p, pl.ds(r0, 8), :] = planes[p]
        return carry

    lax.fori_loop(0, n_tiles, score_body, 0)

    active = lax.broadcasted_iota(jnp.int32, cand_ref.shape, 0) < n_tiles * 8
    cand_ref[...] = jnp.where(active, jnp.int32(-1), jnp.int32(0))

    def radix_body(p, carry):
        k_rem, t_u = carry
        plane = planes_ref[p]
        cand = cand_ref[...]
        cnt = jnp.sum(lax.population_count(cand & plane), axis=0, keepdims=True)
        take = cnt >= k_rem
        cand_ref[...] = cand & (plane ^ jnp.where(take, jnp.int32(0), jnp.int32(-1)))
        bit = lax.shift_left(jnp.int32(1), 31 - p)
        return jnp.where(take, k_rem, k_rem - cnt), jnp.where(take, t_u | bit, t_u)

    _, t_u = lax.fori_loop(0, 32, radix_body,
                           (jnp.full((1, qb), topk, jnp.int32), jnp.zeros((1, qb), jnp.int32)))

    def count_at(thr):
        def body(kt, carry):
            n_gt, n_ge = carry
            x = scores_ref[kt]
            return (n_gt + jnp.sum(jnp.where(x > thr, 1.0, 0.0), axis=0, keepdims=True),
                    n_ge + jnp.sum(jnp.where(x >= thr, 1.0, 0.0), axis=0, keepdims=True))
        zero = jnp.zeros((1, qb), F32)
        return lax.fori_loop(0, n_tiles, body, (zero, zero))

    def publish(thr):
        n_gt, n_ge = count_at(thr)
        sel_ref[0:1, :] = thr
        sel_ref[1:2, :] = n_gt
        sel_ref[2:3, :] = n_ge
        return n_gt, n_ge

    n_gt, n_ge = publish(_finite_threshold(t_u))
    few = (j * qb + lax.broadcasted_iota(jnp.int32, (1, qb), 1)) < topk - 1
    good = jnp.logical_or(few, jnp.logical_and(n_gt < topk, n_ge >= topk))
    all_good = jnp.min(jnp.where(good, 1, 0))

    @pl.when(all_good == 0)
    def _():
        def bit_body(i, pat_u):
            cand_u = pat_u | lax.shift_left(jnp.int32(1), 31 - i)
            cand_f = _pattern_to_float(cand_u)

            def cnt_body(kt, acc):
                return acc + jnp.sum(jnp.where(scores_ref[kt] >= cand_f, 1.0, 0.0), axis=0, keepdims=True)

            cnt = lax.fori_loop(0, n_tiles, cnt_body, jnp.zeros((1, qb), F32))
            return jnp.where(cnt >= topk, cand_u, pat_u)

        pat_u = lax.fori_loop(0, 32, bit_body, jnp.zeros((1, qb), jnp.int32))
        publish(_finite_threshold(pat_u))

    thr = sel_ref[0:1, :]
    need = topk - sel_ref[1:2, :]
    most_ge = jnp.max(sel_ref[2:3, :])

    @pl.when(most_ge <= topk)
    def _():
        def bias_body(kt, carry):
            bias_ref[kt] = jnp.where(scores_ref[kt] >= thr, 0.0, -jnp.inf)
            return carry
        lax.fori_loop(0, n_tiles, bias_body, 0)

    @pl.when(most_ge > topk)
    def _():
        lower = (lax.broadcasted_iota(jnp.int32, (tk, tk), 0) >= lax.broadcasted_iota(jnp.int32, (tk, tk), 1))
        lower = jnp.where(lower, 1.0, 0.0).astype(BF16)

        def bias_body(kt, seen):
            x = scores_ref[kt]
            eq = x == thr
            eqf = jnp.where(eq, 1.0, 0.0).astype(BF16)
            rank = jnp.dot(lower, eqf, preferred_element_type=F32) + seen
            keep_tie = jnp.where(rank <= need, 0.0, -jnp.inf)
            bias_ref[kt] = jnp.where(x > thr, 0.0, jnp.where(eq, keep_tie, -jnp.inf))
            return rank[tk - 1:tk, :]

        lax.fori_loop(0, n_tiles, bias_body, jnp.zeros((1, qb), F32))

    m_ref[...] = jnp.full(m_ref.shape, NEG_INIT, F32)
    acc_ref[...] = jnp.zeros(acc_ref.shape, F32)
    bias_ref[n_tiles] = jnp.full((tk, qb), -jnp.inf, F32)
    max_tile = ckvT_ref.shape[1] - 1

    def produce(kt, s_ref, mt_ref):
        k0 = pl.multiple_of(jnp.minimum(kt, max_tile) * tk, tk)
        s = jnp.dot(ckv_ref[0, pl.ds(k0, tk), :], qlT_ref[...], preferred_element_type=F32)
        s = s + jnp.concatenate([bias_ref[jnp.minimum(kt, n_tiles)]] * n_heads, axis=1)
        s_ref[...] = s
        mt_ref[...] = jnp.max(s, axis=0, keepdims=True)

    def consume(kt, s_ref, mt_ref):
        m_old = m_ref[...]
        m_new = jnp.maximum(m_old, mt_ref[...])
        p = jnp.exp2(s_ref[...] - m_new).astype(BF16)
        acc_ref[...] = acc_ref[...] * jnp.exp2(m_old - m_new) + jnp.dot(
            ckvT_ref[0, jnp.minimum(kt, max_tile)], p, preferred_element_type=F32)
        m_ref[...] = m_new

    produce(0, sa_ref, ma_ref)

    def pair_body(i, carry):
        produce(2 * i + 1, sb_ref, mb_ref)
        consume(2 * i, sa_ref, ma_ref)
        produce(2 * i + 2, sa_ref, ma_ref)
        consume(2 * i + 1, sb_ref, mb_ref)
        return carry

    lax.fori_loop(0, (n_tiles + 1) // 2, pair_body, 0)

    inv_l = 1.0 / acc_ref[n_kv:n_kv + 1, :]
    parts = []
    for h in range(n_heads):
        o_h = (acc_ref[0:n_kv, h * qb:(h + 1) * qb] * inv_l[:, h * qb:(h + 1) * qb]).astype(BF16)
        parts.append(jnp.dot(wuvT_ref[h], o_h, preferred_element_type=F32))
    t_tok = jnp.concatenate(parts, axis=0).T.astype(BF16)
    o_ref[0] = jnp.dot(t_tok, wproj_ref[...], preferred_element_type=F32).astype(BF16)


def _dsa_branch(qT, qiT, wiT, ki, ckv, ckvT, w_uk, w_uv, w_proj, topk):
    bsz, n_q, seq = qT.shape
    n_kv, n_heads, d_head = w_uk.shape
    n_ki = ki.shape[2]
    n_iheads = qiT.shape[1] // n_ki
    d_out = w_proj.shape[1]
    qb, tk = Q_BLOCK, KEY_TILE
    wuk = jnp.transpose(w_uk, (1, 0, 2)).astype(BF16)
    wuvT = jnp.transpose(w_uv, (1, 2, 0)).astype(BF16)
    wproj = w_proj.astype(BF16)
    kern = functools.partial(_dsa_kernel, topk=topk, n_heads=n_heads, d_head=d_head,
                             n_iheads=n_iheads, d_ihead=n_ki, n_kv=n_kv)
    lane = lambda r: pl.BlockSpec((1, r, qb), lambda b, j: (b, 0, j))
    per_b = lambda a: pl.BlockSpec((1,) + a.shape[1:], lambda b, j: (b,) + (0,) * (a.ndim - 1))
    full = lambda a: pl.BlockSpec(a.shape, lambda b, j: (0,) * a.ndim)
    return pl.pallas_call(
        kern,
        grid=(bsz, seq // qb),
        in_specs=[lane(n_q), lane(qiT.shape[1]), lane(wiT.shape[1]), per_b(ki), per_b(ckv), per_b(ckvT),
                  full(wuk), full(wuvT), full(wproj)],
        out_specs=pl.BlockSpec((1, qb, d_out), lambda b, j: (b, j, 0)),
        out_shape=jax.ShapeDtypeStruct((bsz, seq, d_out), BF16),
        scratch_shapes=[pltpu.VMEM((seq // tk, tk, qb), F32),
                        pltpu.VMEM((32, seq // tk * 8, qb), jnp.int32),
                        pltpu.VMEM((seq // tk * 8, qb), jnp.int32),
                        pltpu.VMEM((8, qb), F32),
                        pltpu.VMEM((seq // tk + 1, tk, qb), F32),
                        pltpu.VMEM((n_kv, n_heads * qb), BF16),
                        pltpu.VMEM((n_kv + ONES_ROWS, n_heads * qb), F32),
                        pltpu.VMEM((1, n_heads * qb), F32),
                        pltpu.VMEM((tk, n_heads * qb), F32), pltpu.VMEM((1, n_heads * qb), F32),
                        pltpu.VMEM((tk, n_heads * qb), F32), pltpu.VMEM((1, n_heads * qb), F32)],
        compiler_params=pltpu.CompilerParams(vmem_limit_bytes=VMEM_LIMIT),
        name="dsa",
    )(qT, qiT, wiT, ki, ckv, ckvT, wuk, wuvT, wproj)


def _ffn_kernel(x_ref, y_ref, a_ref, ga_ref, gb_ref, g1_ref, sh2_ref, sc2_ref, g2_ref,
                wglu_ref, bglu_ref, wout_ref, n2_ref, wg_ref, wu_ref, wd_ref, fg_ref, o_ref, *, n_chunks):
    x = x_ref[0]
    z = jnp.dot(y_ref[0], wglu_ref[...], preferred_element_type=F32) + bglu_ref[...]
    d = z.shape[1] // 2
    y_ssm = z[:, :d] * jax.nn.sigmoid(z[:, d:])
    merged = (jax.nn.sigmoid(ga_ref[0].astype(F32)) * y_ssm
              + jax.nn.sigmoid(gb_ref[0].astype(F32)) * a_ref[0].astype(F32))
    h1 = x + g1_ref[0] * jnp.dot(merged.astype(BF16), wout_ref[...], preferred_element_type=F32)
    u2 = _rms(h1, -1) * n2_ref[...]
    u2 = (u2 * (1.0 + sc2_ref[0]) + sh2_ref[0]).astype(BF16)
    hidden = wg_ref.shape[1]
    step = hidden // n_chunks
    ffn = None
    for ci in range(n_chunks):
        sl = slice(ci * step, (ci + 1) * step)
        gate = jnp.dot(u2, wg_ref[:, sl], preferred_element_type=F32)
        up = jnp.dot(u2, wu_ref[:, sl], preferred_element_type=F32)
        act = ((gate * jax.nn.sigmoid(gate)) * up).astype(BF16)
        part = jnp.dot(act, wd_ref[sl, :], preferred_element_type=F32)
        ffn = part if ffn is None else ffn + part
    h2 = h1 + g2_ref[0] * ffn
    o_ref[0] = _rms(h2, -1) * fg_ref[...]


def _merge_ffn(x, y_gelu, y_att, ga, gb, gate1, shift2, scale2, gate2,
               w_glu, b_glu, w_out, norm2_g, w_gate, w_up, w_down, final_g):
    bsz, seq, d = x.shape
    tm = TOKEN_TILE
    hidden = w_gate.shape[1]
    n_chunks = 2 if hidden % 256 == 0 else 1
    tok = lambda w: pl.BlockSpec((1, tm, w), lambda b, t: (b, t, 0))
    vec = lambda w: pl.BlockSpec((1, 1, w), lambda b, t: (b, 0, 0))
    const = lambda a: pl.BlockSpec(a.shape, lambda b, t: (0,) * a.ndim, pipeline_mode=pl.Buffered(1))
    wglu = w_glu.astype(BF16)
    bglu = b_glu.reshape(1, -1)
    wout = w_out.astype(BF16)
    n2 = norm2_g.reshape(1, d)
    wg = w_gate.astype(BF16)
    wu = w_up.astype(BF16)
    wd = w_down.astype(BF16)
    fg = final_g.reshape(1, d)
    return pl.pallas_call(
        functools.partial(_ffn_kernel, n_chunks=n_chunks),
        grid=(bsz, seq // tm),
        in_specs=[tok(d), tok(y_gelu.shape[2]), tok(d), tok(d), tok(d), vec(d), vec(d), vec(d), vec(d),
                  const(wglu), const(bglu), const(wout), const(n2), const(wg), const(wu), const(wd), const(fg)],
        out_specs=tok(d),
        out_shape=jax.ShapeDtypeStruct((bsz, seq, d), F32),
        compiler_params=pltpu.CompilerParams(vmem_limit_bytes=VMEM_LIMIT),
        name="merge_ffn",
    )(x, y_gelu, y_att, ga, gb, gate1, shift2, scale2, gate2, wglu, bglu, wout, n2, wg, wu, wd, fg)


def kernel(x, c, w_mod, b_mod, norm1_g, w_in, ssm_a_re, ssm_a_im, ssm_log_dt, ssm_b_re, ssm_b_im, ssm_c_re, ssm_c_im, ssm_d, w_ssm_glu, b_ssm_glu, kv_norm_g, idx_k_norm_g, w_uk, w_uv, w_attn_proj, w_out, norm2_g, w_ffn_gate, w_ffn_up, w_ffn_down, final_g):
    bsz, seq, d = x.shape
    assert w_mod.shape[0] == 1, "single-layer block"
    assert seq % TOKEN_TILE == 0 and seq % KEY_TILE == 0 and TOKEN_TILE % KEY_TILE == 0
    g, p = ssm_a_re.shape[1:]
    cg = ssm_b_re.shape[3]
    assert cg == SSM_GROUP and g % 2 == 0
    n_xs = g * cg
    n_kv, n_heads, d_head = w_uk.shape[1:]
    n_q = n_heads * d_head
    n_ki = idx_k_norm_g.shape[1]
    n_wi = w_in.shape[2] - (n_xs + n_q + n_kv + n_ki + 2 * d)
    n_wi = n_wi // (n_ki + 1)
    n_qi = n_wi * n_ki
    topk = min(TOPK_MAX, seq // 4)

    mod = _modulation(c, w_mod[0], b_mod[0])
    shift1, scale1, gate1, shift2, scale2, gate2 = [m.reshape(bsz, 1, d) for m in jnp.split(mod, 6, axis=-1)]

    offs = {}
    o = 0
    for name, size in (("xs", n_xs), ("q", n_q), ("ckv", n_kv), ("qi", n_qi), ("ki", n_ki), ("wi", n_wi),
                       ("ga", d), ("gb", d)):
        offs[name] = (o, o + size)
        o += size
    assert o == w_in.shape[2]
    wi_all = w_in[0]
    cols = lambda n: wi_all[:, offs[n][0]:offs[n][1]]
    wi_rows = -(-n_wi // 8) * 8
    w_nat = jnp.concatenate([cols("xs"), cols("ga"), cols("gb"), cols("ckv"), cols("ki")], axis=1).astype(BF16)
    w_tr = jnp.concatenate([cols("q"), cols("qi"), cols("ckv"),
                            jnp.pad(cols("wi"), ((0, 0), (0, wi_rows - n_wi)))], axis=1).T.astype(BF16)
    xs, ga, gb, ckv, ki, qT, qiT, ckvT, wiT = _in_projection(
        x, scale1, shift1, norm1_g[0], w_nat, w_tr, kv_norm_g[0], idx_k_norm_g[0],
        (n_xs, d, n_kv, n_ki, n_q, n_qi, wi_rows))

    prep = _ssm_prep(ssm_a_re[0], ssm_a_im[0], ssm_log_dt[0], ssm_b_re[0], ssm_b_im[0], ssm_c_re[0], ssm_c_im[0])
    y_gelu = _s5_branch(xs, _ssm_matrices(*prep, ssm_d[0]))

    y_att = _dsa_branch(qT, qiT, wiT, ki, ckv, ckvT, w_uk[0], w_uv[0], w_attn_proj[0], topk)

    return _merge_ffn(x, y_gelu, y_att, ga, gb, gate1, shift2, scale2, gate2,
                      w_ssm_glu[0], b_ssm_glu[0], w_out[0], norm2_g[0],
                      w_ffn_gate[0], w_ffn_up[0], w_ffn_down[0], final_g)
```

```python
import functools

import jax
import jax.numpy as jnp
from jax import lax
from jax.experimental import pallas as pl
from jax.experimental.pallas import tpu as pltpu

F32 = jnp.float32
BF16 = jnp.bfloat16
HIGHEST = lax.Precision.HIGHEST

EPS = 1e-6
SSM_GROUP = 16
SSM_CHUNK = 16
TOPK_MAX = 256
LANES = 128
Q_BLOCK = 128
KEY_TILE = 256
TOKEN_TILE = 512
ONES_ROWS = 8
INT_MIN = -(2 ** 31)
NEG_INIT = -1e30
LOG2E = 1.4426950408889634
VMEM_LIMIT = 56 * 1024 * 1024


def _rms(x, axis):
    return x * lax.rsqrt(jnp.mean(x * x, axis=axis, keepdims=True) + EPS)


def _mod_kernel(c_ref, w_ref, b_ref, o_ref):
    cv = c_ref[...]
    cond = cv * jax.nn.sigmoid(cv)
    o_ref[...] = jnp.dot(cond, w_ref[...], precision=HIGHEST, preferred_element_type=F32) + b_ref[...]


def _modulation(c, w_mod, b_mod):
    bsz, d = c.shape
    n = w_mod.shape[1]
    rows = -(-bsz // 8) * 8
    cp = jnp.pad(c, ((0, rows - bsz), (0, 0)))
    tn = d
    out = pl.pallas_call(
        _mod_kernel,
        grid=(n // tn,),
        in_specs=[pl.BlockSpec((rows, d), lambda j: (0, 0)),
                  pl.BlockSpec((d, tn), lambda j: (0, j)),
                  pl.BlockSpec((1, tn), lambda j: (0, j))],
        out_specs=pl.BlockSpec((rows, tn), lambda j: (0, j)),
        out_shape=jax.ShapeDtypeStruct((rows, n), F32),
        name="modulation",
    )(cp, w_mod, b_mod.reshape(1, n))
    return out[:bsz]


def _inproj_kernel(x_ref, sc_ref, sh_ref, g1_ref, wn_ref, wt_ref, gkv_ref, gkvc_ref, gki_ref,
                   xs_ref, ga_ref, gb_ref, ckv_ref, ki_ref, qT_ref, qiT_ref, ckvT_ref, wiT_ref,
                   *, n_xs, n_gate, n_kv, n_ki, n_q, n_qi, n_wi):
    x = x_ref[0]
    u = _rms(x, -1) * g1_ref[...]
    u = u * (1.0 + sc_ref[0]) + sh_ref[0]
    ub = u.astype(BF16)
    tm = ub.shape[0]

    zn = jnp.dot(ub, wn_ref[...], preferred_element_type=F32)
    o = 0
    for s in range(n_xs // LANES):
        xs_ref[0, s] = zn[:, s * LANES:(s + 1) * LANES]
    o += n_xs
    ga_ref[0] = zn[:, o:o + n_gate].astype(BF16)
    o += n_gate
    gb_ref[0] = zn[:, o:o + n_gate].astype(BF16)
    o += n_gate
    ckv_ref[0] = (_rms(zn[:, o:o + n_kv], -1) * gkv_ref[...]).astype(BF16)
    o += n_kv
    ki_ref[0] = (_rms(zn[:, o:o + n_ki], -1) * gki_ref[...]).astype(BF16)

    zt = lax.dot_general(wt_ref[...], ub, (((1,), (1,)), ((), ())), preferred_element_type=F32)
    o = 0
    qT_ref[0] = zt[o:o + n_q].astype(BF16)
    o += n_q
    qiT_ref[0] = zt[o:o + n_qi].astype(BF16)
    o += n_qi
    ct = (_rms(zt[o:o + n_kv], 0) * gkvc_ref[...]).astype(BF16)
    o += n_kv
    wiT_ref[0] = zt[o:o + n_wi]
    ones_blk = jnp.where(lax.broadcasted_iota(jnp.int32, (ONES_ROWS, KEY_TILE), 0) == 0, 1.0, 0.0).astype(BF16)
    for i in range(tm // KEY_TILE):
        ckvT_ref[0, i, 0:n_kv, :] = ct[:, i * KEY_TILE:(i + 1) * KEY_TILE]
        ckvT_ref[0, i, n_kv:n_kv + ONES_ROWS, :] = ones_blk


def _in_projection(x, scale1, shift1, norm1_g, w_nat, w_tr, kv_g, ki_g, sizes):
    bsz, seq, d = x.shape
    n_xs, n_gate, n_kv, n_ki, n_q, n_qi, n_wi = sizes
    tm = TOKEN_TILE
    nt = seq // tm
    kern = functools.partial(_inproj_kernel, n_xs=n_xs, n_gate=n_gate, n_kv=n_kv, n_ki=n_ki,
                             n_q=n_q, n_qi=n_qi, n_wi=n_wi)
    tok = lambda w: pl.BlockSpec((1, tm, w), lambda b, t: (b, t, 0))
    lane = lambda r: pl.BlockSpec((1, r, tm), lambda b, t: (b, 0, t))
    vec = lambda w: pl.BlockSpec((1, 1, w), lambda b, t: (b, 0, 0))
    full = lambda a: pl.BlockSpec(a.shape, lambda b, t: (0,) * a.ndim)
    g1 = norm1_g.reshape(1, d)
    gkv = kv_g.reshape(1, n_kv)
    gkvc = kv_g.reshape(n_kv, 1)
    gki = ki_g.reshape(1, n_ki)
    out_shape = (
        jax.ShapeDtypeStruct((bsz, n_xs // LANES, seq, LANES), F32),
        jax.ShapeDtypeStruct((bsz, seq, n_gate), BF16),
        jax.ShapeDtypeStruct((bsz, seq, n_gate), BF16),
        jax.ShapeDtypeStruct((bsz, seq, n_kv), BF16),
        jax.ShapeDtypeStruct((bsz, seq, n_ki), BF16),
        jax.ShapeDtypeStruct((bsz, n_q, seq), BF16),
        jax.ShapeDtypeStruct((bsz, n_qi, seq), BF16),
        jax.ShapeDtypeStruct((bsz, seq // KEY_TILE, n_kv + ONES_ROWS, KEY_TILE), BF16),
        jax.ShapeDtypeStruct((bsz, n_wi, seq), F32),
    )
    out_specs = (
        pl.BlockSpec((1, n_xs // LANES, tm, LANES), lambda b, t: (b, 0, t, 0)),
        tok(n_gate), tok(n_gate), tok(n_kv), tok(n_ki),
        lane(n_q), lane(n_qi),
        pl.BlockSpec((1, tm // KEY_TILE, n_kv + ONES_ROWS, KEY_TILE), lambda b, t: (b, t, 0, 0)),
        lane(n_wi),
    )
    return pl.pallas_call(
        kern,
        grid=(bsz, nt),
        in_specs=[tok(d), vec(d), vec(d), full(g1), full(w_nat), full(w_tr), full(gkv), full(gkvc), full(gki)],
        out_specs=out_specs,
        out_shape=out_shape,
        compiler_params=pltpu.CompilerParams(vmem_limit_bytes=VMEM_LIMIT),
        name="in_projection",
    )(x, scale1, shift1, g1, w_nat, w_tr, gkv, gkvc, gki)


def _ssm_prep_kernel(are_ref, aim_ref, ldt_ref, bre_ref, bim_ref, cre_ref, cim_ref, ctre_ref, ctim_ref,
                     kall_ref, dre_ref, dim_ref, vre_ref, vimn_ref, lre_ref, lim_ref):
    ar = are_ref[0]
    ai = aim_ref[0]
    dt = jnp.exp(ldt_ref[0])
    ard = ar * dt
    aid = ai * dt

    def lam_pow(kf):
        mag = jnp.exp(ard * kf)
        ang = aid * kf
        return mag * jnp.cos(ang), mag * jnp.sin(ang)

    lb_re, lb_im = lam_pow(1.0)
    den = ar * ar + ai * ai
    coef_re = ((lb_re - 1.0) * ar + lb_im * ai) / den
    coef_im = (lb_im * ar - (lb_re - 1.0) * ai) / den
    bre = bre_ref[0]
    bim = bim_ref[0]
    bb_re = coef_re * bre - coef_im * bim
    bb_im = coef_re * bim + coef_im * bre

    width = bre.shape[1]
    lag = lax.shift_right_logical(lax.broadcasted_iota(jnp.int32, (1, width), 1), 4).astype(F32)
    lk_re, lk_im = lam_pow(lag)
    d_re = lk_re * bb_re - lk_im * bb_im
    d_im = lk_re * bb_im + lk_im * bb_re
    dre_ref[0] = d_re
    dim_ref[0] = d_im
    kall_ref[0] = (jnp.dot(cre_ref[0], d_re, precision=HIGHEST, preferred_element_type=F32)
                   - jnp.dot(cim_ref[0], d_im, precision=HIGHEST, preferred_element_type=F32))
    l1_re, l1_im = lam_pow(lag + 1.0)
    ctre = ctre_ref[0]
    ctim = ctim_ref[0]
    vre_ref[0] = l1_re * ctre - l1_im * ctim
    vimn_ref[0] = -(l1_re * ctim + l1_im * ctre)
    lt_re, lt_im = lam_pow(float(SSM_CHUNK))
    lre_ref[0] = lt_re
    lim_ref[0] = lt_im


def _ssm_prep(a_re, a_im, log_dt, b_re, b_im, c_re, c_im):
    g, p = a_re.shape
    cg = b_re.shape[2]
    width = SSM_CHUNK * cg
    col = lambda a: a.reshape(g, p, 1)
    b_t = lambda b: jnp.tile(b, (1, 1, SSM_CHUNK))
    c_t = lambda cc: jnp.tile(jnp.swapaxes(cc, 1, 2), (1, 1, SSM_CHUNK))
    blk = lambda s: pl.BlockSpec((1,) + s, lambda i: (i, 0, 0))
    outs = pl.pallas_call(
        _ssm_prep_kernel,
        grid=(g,),
        in_specs=[blk((p, 1)), blk((p, 1)), blk((1, 1)), blk((p, width)), blk((p, width)),
                  blk((cg, p)), blk((cg, p)), blk((p, width)), blk((p, width))],
        out_specs=(blk((cg, width)), blk((p, width)), blk((p, width)), blk((p, width)), blk((p, width)),
                   blk((p, 1)), blk((p, 1))),
        out_shape=(jax.ShapeDtypeStruct((g, cg, width), F32),) + (jax.ShapeDtypeStruct((g, p, width), F32),) * 4
        + (jax.ShapeDtypeStruct((g, p, 1), F32),) * 2,
        name="ssm_prep",
    )(col(a_re), col(a_im), log_dt.reshape(g, 1, 1), b_t(b_re), b_t(b_im), c_re, c_im, c_t(c_re), c_t(c_im))
    return outs


def _ssm_matrices(kall, d_re, d_im, v_re, v_imn, l_re, l_im, d_skip):
    g, cg, width = kall.shape
    p = d_re.shape[1]
    t = SSM_CHUNK
    gs = LANES // cg
    ns = g // gs
    k4 = kall.reshape(g, cg, t, cg)
    tt = jnp.arange(t)[:, None]
    jj = jnp.arange(t)[None, :]
    lagidx = jnp.clip(tt - jj, 0, t - 1)
    m = k4[:, :, lagidx, :]
    m = jnp.where((tt >= jj)[None, None, :, :, None], m, 0.0)
    m = jnp.transpose(m, (0, 2, 1, 3, 4)).reshape(ns, gs, width, width).astype(BF16)
    w_of = lambda d: d.reshape(g, p, t, cg)[:, :, ::-1, :].reshape(g, p, width)
    w = jnp.concatenate([w_of(d_re), w_of(d_im)], axis=1).reshape(ns, gs, 2 * p, width).astype(BF16)
    v = jnp.concatenate([jnp.swapaxes(v_re, 1, 2), jnp.swapaxes(v_imn, 1, 2)], axis=2)
    v = v.reshape(ns, gs, width, 2 * p).astype(BF16)
    lam = jnp.concatenate([l_re.reshape(ns, 1, gs * p), l_im.reshape(ns, 1, gs * p)], axis=1)
    dsk = jnp.broadcast_to(d_skip.reshape(g, 1, cg), (g, t, cg)).reshape(ns, gs, width, 1)
    return m, w, v, lam, dsk


def _gelu_tanh(y):
    return 0.5 * y * (1.0 + jnp.tanh(0.7978845608028654 * (y + 0.044715 * (y * y * y))))


def _ssm_kernel(xs_ref, m_ref, w_ref, v_ref, lam_ref, d_ref, y_ref,
                zt_ref, ure_ref, uim_ref, sre_ref, sim_ref, yt_ref):
    t = SSM_CHUNK
    cg = SSM_GROUP
    gs = zt_ref.shape[0]
    nch = zt_ref.shape[2]
    p = w_ref.shape[2] // 2

    for ti in range(t):
        xt = xs_ref[0, 0, pl.ds(ti, nch, stride=t), :].T
        for g in range(gs):
            zt_ref[g, ti * cg:(ti + 1) * cg, :] = xt[g * cg:(g + 1) * cg, :].astype(BF16)

    ut = [jnp.dot(w_ref[0, g], zt_ref[g], preferred_element_type=F32) for g in range(gs)]
    ure_ref[...] = jnp.concatenate([u[:p] for u in ut], axis=0).T
    uim_ref[...] = jnp.concatenate([u[p:] for u in ut], axis=0).T

    lr = lam_ref[0, 0:1, :]
    li = lam_ref[0, 1:2, :]

    def scan_body(i, carry):
        s_re, s_im = carry
        sre_ref[pl.ds(i, 1), :] = s_re
        sim_ref[pl.ds(i, 1), :] = s_im
        u_re = ure_ref[pl.ds(i, 1), :]
        u_im = uim_ref[pl.ds(i, 1), :]
        return (lr * s_re - li * s_im + u_re, lr * s_im + li * s_re + u_im)

    zero = jnp.zeros_like(lr)
    lax.fori_loop(0, nch, scan_body, (zero, zero))

    sre_t = sre_ref[...].T
    sim_t = sim_ref[...].T
    for g in range(gs):
        z = zt_ref[g]
        s = jnp.concatenate([sre_t[g * p:(g + 1) * p], sim_t[g * p:(g + 1) * p]], axis=0).astype(BF16)
        y = (jnp.dot(m_ref[0, g], z, preferred_element_type=F32)
             + jnp.dot(v_ref[0, g], s, preferred_element_type=F32)
             + d_ref[0, g] * z.astype(F32))
        yt_ref[g] = _gelu_tanh(y)

    for ti in range(t):
        blk = jnp.concatenate([yt_ref[g, ti * cg:(ti + 1) * cg, :] for g in range(gs)], axis=0)
        y_ref[0, 0, pl.ds(ti, nch, stride=t), :] = blk.T


def _s5_branch(xs, mats):
    m, w, v, lam, dsk = mats
    bsz, ns, seq, lanes = xs.shape
    gs = m.shape[1]
    width = m.shape[2]
    p2 = w.shape[2]
    nch = seq // SSM_CHUNK
    slab = pl.BlockSpec((1, 1, seq, lanes), lambda b, s: (b, s, 0, 0))
    per_slab = lambda a: pl.BlockSpec((1,) + a.shape[1:], lambda b, s: (s,) + (0,) * (a.ndim - 1))
    return pl.pallas_call(
        _ssm_kernel,
        grid=(bsz, ns),
        in_specs=[slab, per_slab(m), per_slab(w), per_slab(v), per_slab(lam), per_slab(dsk)],
        out_specs=slab,
        out_shape=jax.ShapeDtypeStruct(xs.shape, F32),
        scratch_shapes=[pltpu.VMEM((gs, width, nch), BF16),
                        pltpu.VMEM((nch, gs * p2 // 2), F32), pltpu.VMEM((nch, gs * p2 // 2), F32),
                        pltpu.VMEM((nch, gs * p2 // 2), F32), pltpu.VMEM((nch, gs * p2 // 2), F32),
                        pltpu.VMEM((gs, width, nch), F32)],
        compiler_params=pltpu.CompilerParams(vmem_limit_bytes=VMEM_LIMIT),
        name="ssm",
    )(xs, m, w, v, lam, dsk)


def _i32(v):
    return jnp.int32(v - (1 << 32) if v >= (1 << 31) else v)


def _order_pattern(x):
    bits = pltpu.bitcast(x, jnp.int32)
    return bits ^ (lax.shift_right_arithmetic(bits, 31) | jnp.int32(INT_MIN))


def _pattern_to_float(u):
    bits = jnp.where(u < 0, u & jnp.int32(0x7FFFFFFF), ~u)
    return pltpu.bitcast(bits, F32)


def _finite_threshold(u):
    return _pattern_to_float(jnp.where(u < 0, u, jnp.maximum(u, jnp.int32(0x00800000))))


def _bit_transpose32(words):
    a = list(words)
    j, m = 16, 0x0000FFFF
    while j:
        k = 0
        while k < 32:
            t = (a[k] ^ lax.shift_right_logical(a[k + j], j)) & _i32(m)
            a[k] = a[k] ^ t
            a[k + j] = a[k + j] ^ lax.shift_left(t, j)
            k = (k + j + 1) & ~j
        j >>= 1
        m = (m ^ (m << j)) & 0xFFFFFFFF
    return a


def _dsa_kernel(qT_ref, qiT_ref, wiT_ref, ki_ref, ckv_ref, ckvT_ref, wuk_ref, wuvT_ref, wproj_ref,
                o_ref, scores_ref, planes_ref, cand_ref, sel_ref, bias_ref, qlT_ref, acc_ref, m_ref,
                sa_ref, ma_ref, sb_ref, mb_ref,
                *, topk, n_heads, d_head, n_iheads, d_ihead, n_kv):
    qb = Q_BLOCK
    tk = KEY_TILE
    words = tk // 8
    j = pl.program_id(1)
    n_tiles = (j * qb + qb + tk - 1) // tk

    for h in range(n_heads):
        qh = qT_ref[0, h * d_head:(h + 1) * d_head, :]
        ql = jnp.dot(wuk_ref[h], qh, preferred_element_type=F32) * (d_head ** -0.5 * LOG2E)
        qlT_ref[:, h * qb:(h + 1) * qb] = ql.astype(BF16)
    qi_cat = jnp.concatenate([qiT_ref[0, h * d_ihead:(h + 1) * d_ihead, :] for h in range(n_iheads)], axis=1)
    w_heads = wiT_ref[0] * (n_iheads ** -0.5 * d_ihead ** -0.5)

    q_pos = j * qb + lax.broadcasted_iota(jnp.int32, (tk, qb), 1)
    row_iota = lax.broadcasted_iota(jnp.int32, (tk, qb), 0)

    def score_body(kt, carry):
        k0 = pl.multiple_of(kt * tk, tk)
        rel = jnp.dot(ki_ref[0, pl.ds(k0, tk), :], qi_cat, preferred_element_type=F32)
        sc = None
        for h in range(n_iheads):
            term = jnp.maximum(rel[:, h * qb:(h + 1) * qb], 0.0) * w_heads[h:h + 1, :]
            sc = term if sc is None else sc + term
        sc = jnp.where((k0 + row_iota) <= q_pos, sc, -jnp.inf)
        scores_ref[kt] = sc
        pat = _order_pattern(sc)
        planes = _bit_transpose32([pat[8 * v:8 * v + 8] for v in range(words)])
        r0 = pl.multiple_of(kt * 8, 8)
        for p in range(32):
            planes_ref[p, pl.ds(r0, 8), :] = planes[p]
        return carry

    @pl.when(jnp.logical_and(pl.program_id(0) == 0, j == 0))
    def _():
        planes_ref[...] = jnp.zeros(planes_ref.shape, jnp.int32)

    lax.fori_loop(0, n_tiles, score_body, 0)

    active = lax.broadcasted_iota(jnp.int32, cand_ref.shape, 0) < n_tiles * 8
    cand_ref[...] = jnp.where(active, jnp.int32(-1), jnp.int32(0))

    def radix_body(p, carry):
        k_rem, t_u = carry
        plane = planes_ref[p]
        cand = cand_ref[...]
        cnt = jnp.sum(lax.population_count(cand & plane), axis=0, keepdims=True)
        take = cnt >= k_rem
        cand_ref[...] = cand & (plane ^ jnp.where(take, jnp.int32(0), jnp.int32(-1)))
        bit = lax.shift_left(jnp.int32(1), 31 - p)
        return jnp.where(take, k_rem, k_rem - cnt), jnp.where(take, t_u | bit, t_u)

    _, t_u = lax.fori_loop(0, 32, radix_body,
                           (jnp.full((1, qb), topk, jnp.int32), jnp.zeros((1, qb), jnp.int32)))

    def count_at(thr):
        def body(kt, carry):
            n_gt, n_ge = carry
            x = scores_ref[kt]
            return (n_gt + jnp.sum(jnp.where(x > thr, 1.0, 0.0), axis=0, keepdims=True),
                    n_ge + jnp.sum(jnp.where(x >= thr, 1.0, 0.0), axis=0, keepdims=True))
        zero = jnp.zeros((1, qb), F32)
        return lax.fori_loop(0, n_tiles, body, (zero, zero))

    def publish(thr):
        n_gt, n_ge = count_at(thr)
        sel_ref[0:1, :] = thr
        sel_ref[1:2, :] = n_gt
        sel_ref[2:3, :] = n_ge
        return n_gt, n_ge

    n_gt, n_ge = publish(_finite_threshold(t_u))
    few = (j * qb + lax.broadcasted_iota(jnp.int32, (1, qb), 1)) < topk - 1
    good = jnp.logical_or(few, jnp.logical_and(n_gt < topk, n_ge >= topk))
    all_good = jnp.min(jnp.where(good, 1, 0))

    @pl.when(all_good == 0)
    def _():
        def bit_body(i, pat_u):
            cand_u = pat_u | lax.shift_left(jnp.int32(1), 31 - i)
            cand_f = _pattern_to_float(cand_u)

            def cnt_body(kt, acc):
                return acc + jnp.sum(jnp.where(scores_ref[kt] >= cand_f, 1.0, 0.0), axis=0, keepdims=True)

            cnt = lax.fori_loop(0, n_tiles, cnt_body, jnp.zeros((1, qb), F32))
            return jnp.where(cnt >= topk, cand_u, pat_u)

        pat_u = lax.fori_loop(0, 32, bit_body, jnp.zeros((1, qb), jnp.int32))
        publish(_finite_threshold(pat_u))

    thr = sel_ref[0:1, :]
    need = topk - sel_ref[1:2, :]
    most_ge = jnp.max(sel_ref[2:3, :])

    @pl.when(most_ge <= topk)
    def _():
        def bias_body(kt, carry):
            bias_ref[kt] = jnp.where(scores_ref[kt] >= thr, 0.0, -jnp.inf)
            return carry
        lax.fori_loop(0, n_tiles, bias_body, 0)

    @pl.when(most_ge > topk)
    def _():
        lower = (lax.broadcasted_iota(jnp.int32, (tk, tk), 0) >= lax.broadcasted_iota(jnp.int32, (tk, tk), 1))
        lower = jnp.where(lower, 1.0, 0.0).astype(BF16)

        def bias_body(kt, seen):
            x = scores_ref[kt]
            eq = x == thr
            eqf = jnp.where(eq, 1.0, 0.0).astype(BF16)
            rank = jnp.dot(lower, eqf, preferred_element_type=F32) + seen
            keep_tie = jnp.where(rank <= need, 0.0, -jnp.inf)
            bias_ref[kt] = jnp.where(x > thr, 0.0, jnp.where(eq, keep_tie, -jnp.inf))
            return rank[tk - 1:tk, :]

        lax.fori_loop(0, n_tiles, bias_body, jnp.zeros((1, qb), F32))

    m_ref[...] = jnp.full(m_ref.shape, NEG_INIT, F32)
    acc_ref[...] = jnp.zeros(acc_ref.shape, F32)
    bias_ref[n_tiles] = jnp.full((tk, qb), -jnp.inf, F32)
    max_tile = ckvT_ref.shape[1] - 1

    def produce(kt, s_ref, mt_ref):
        k0 = pl.multiple_of(jnp.minimum(kt, max_tile) * tk, tk)
        s = jnp.dot(ckv_ref[0, pl.ds(k0, tk), :], qlT_ref[...], preferred_element_type=F32)
        s = s + jnp.concatenate([bias_ref[jnp.minimum(kt, n_tiles)]] * n_heads, axis=1)
        s_ref[...] = s
        mt_ref[...] = jnp.max(s, axis=0, keepdims=True)

    def consume(kt, s_ref, mt_ref):
        m_old = m_ref[...]
        m_new = jnp.maximum(m_old, mt_ref[...])
        p = jnp.exp2(s_ref[...] - m_new).astype(BF16)
        acc_ref[...] = acc_ref[...] * jnp.exp2(m_old - m_new) + jnp.dot(
            ckvT_ref[0, jnp.minimum(kt, max_tile)], p, preferred_element_type=F32)
        m_ref[...] = m_new

    produce(0, sa_ref, ma_ref)

    def pair_body(i, carry):
        produce(2 * i + 1, sb_ref, mb_ref)
        consume(2 * i, sa_ref, ma_ref)
        produce(2 * i + 2, sa_ref, ma_ref)
        consume(2 * i + 1, sb_ref, mb_ref)
        return carry

    lax.fori_loop(0, (n_tiles + 1) // 2, pair_body, 0)

    inv_l = 1.0 / acc_ref[n_kv:n_kv + 1, :]
    parts = []
    for h in range(n_heads):
        o_h = (acc_ref[0:n_kv, h * qb:(h + 1) * qb] * inv_l[:, h * qb:(h + 1) * qb]).astype(BF16)
        parts.append(jnp.dot(wuvT_ref[h], o_h, preferred_element_type=F32))
    t_tok = jnp.concatenate(parts, axis=0).T.astype(BF16)
    o_ref[0] = jnp.dot(t_tok, wproj_ref[...], preferred_element_type=F32).astype(BF16)


def _dsa_branch(qT, qiT, wiT, ki, ckv, ckvT, w_uk, w_uv, w_proj, topk):
    bsz, n_q, seq = qT.shape
    n_kv, n_heads, d_head = w_uk.shape
    n_ki = ki.shape[2]
    n_iheads = qiT.shape[1] // n_ki
    d_out = w_proj.shape[1]
    qb, tk = Q_BLOCK, KEY_TILE
    wuk = jnp.transpose(w_uk, (1, 0, 2)).astype(BF16)
    wuvT = jnp.transpose(w_uv, (1, 2, 0)).astype(BF16)
    wproj = w_proj.astype(BF16)
    kern = functools.partial(_dsa_kernel, topk=topk, n_heads=n_heads, d_head=d_head,
                             n_iheads=n_iheads, d_ihead=n_ki, n_kv=n_kv)
    lane = lambda r: pl.BlockSpec((1, r, qb), lambda b, j: (b, 0, j))
    per_b = lambda a: pl.BlockSpec((1,) + a.shape[1:], lambda b, j: (b,) + (0,) * (a.ndim - 1))
    full = lambda a: pl.BlockSpec(a.shape, lambda b, j: (0,) * a.ndim)
    return pl.pallas_call(
        kern,
        grid=(bsz, seq // qb),
        in_specs=[lane(n_q), lane(qiT.shape[1]), lane(wiT.shape[1]), per_b(ki), per_b(ckv), per_b(ckvT),
                  full(wuk), full(wuvT), full(wproj)],
        out_specs=pl.BlockSpec((1, qb, d_out), lambda b, j: (b, j, 0)),
        out_shape=jax.ShapeDtypeStruct((bsz, seq, d_out), BF16),
        scratch_shapes=[pltpu.VMEM((seq // tk, tk, qb), F32),
                        pltpu.VMEM((32, seq // tk * 8, qb), jnp.int32),
                        pltpu.VMEM((seq // tk * 8, qb), jnp.int32),
                        pltpu.VMEM((8, qb), F32),
                        pltpu.VMEM((seq // tk + 1, tk, qb), F32),
                        pltpu.VMEM((n_kv, n_heads * qb), BF16),
                        pltpu.VMEM((n_kv + ONES_ROWS, n_heads * qb), F32),
                        pltpu.VMEM((1, n_heads * qb), F32),
                        pltpu.VMEM((tk, n_heads * qb), F32), pltpu.VMEM((1, n_heads * qb), F32),
                        pltpu.VMEM((tk, n_heads * qb), F32), pltpu.VMEM((1, n_heads * qb), F32)],
        compiler_params=pltpu.CompilerParams(vmem_limit_bytes=VMEM_LIMIT),
        name="dsa",
    )(qT, qiT, wiT, ki, ckv, ckvT, wuk, wuvT, wproj)


def _ffn_kernel(x_ref, y_ref, a_ref, ga_ref, gb_ref, g1_ref, sh2_ref, sc2_ref, g2_ref,
                wglu_ref, bglu_ref, wout_ref, n2_ref, wg_ref, wu_ref, wd_ref, fg_ref, o_ref, *, n_chunks):
    x = x_ref[0]
    y_gelu = jnp.concatenate([y_ref[0, s] for s in range(y_ref.shape[1])], axis=1).astype(BF16)
    z = jnp.dot(y_gelu, wglu_ref[...], preferred_element_type=F32) + bglu_ref[...]
    d = z.shape[1] // 2
    y_ssm = z[:, :d] * jax.nn.sigmoid(z[:, d:])
    merged = (jax.nn.sigmoid(ga_ref[0].astype(F32)) * y_ssm
              + jax.nn.sigmoid(gb_ref[0].astype(F32)) * a_ref[0].astype(F32))
    h1 = x + g1_ref[0] * jnp.dot(merged.astype(BF16), wout_ref[...], preferred_element_type=F32)
    u2 = _rms(h1, -1) * n2_ref[...]
    u2 = (u2 * (1.0 + sc2_ref[0]) + sh2_ref[0]).astype(BF16)
    hidden = wg_ref.shape[1]
    step = hidden // n_chunks
    ffn = None
    for ci in range(n_chunks):
        sl = slice(ci * step, (ci + 1) * step)
        gate = jnp.dot(u2, wg_ref[:, sl], preferred_element_type=F32)
        up = jnp.dot(u2, wu_ref[:, sl], preferred_element_type=F32)
        act = ((gate * jax.nn.sigmoid(gate)) * up).astype(BF16)
        part = jnp.dot(act, wd_ref[sl, :], preferred_element_type=F32)
        ffn = part if ffn is None else ffn + part
    h2 = h1 + g2_ref[0] * ffn
    o_ref[0] = _rms(h2, -1) * fg_ref[...]


def _merge_ffn(x, y_gelu, y_att, ga, gb, gate1, shift2, scale2, gate2,
               w_glu, b_glu, w_out, norm2_g, w_gate, w_up, w_down, final_g):
    bsz, seq, d = x.shape
    tm = TOKEN_TILE
    hidden = w_gate.shape[1]
    n_chunks = 2 if hidden % 256 == 0 else 1
    tok = lambda w: pl.BlockSpec((1, tm, w), lambda b, t: (b, t, 0))
    vec = lambda w: pl.BlockSpec((1, 1, w), lambda b, t: (b, 0, 0))
    const = lambda a: pl.BlockSpec(a.shape, lambda b, t: (0,) * a.ndim, pipeline_mode=pl.Buffered(1))
    wglu = w_glu.astype(BF16)
    bglu = b_glu.reshape(1, -1)
    wout = w_out.astype(BF16)
    n2 = norm2_g.reshape(1, d)
    wg = w_gate.astype(BF16)
    wu = w_up.astype(BF16)
    wd = w_down.astype(BF16)
    fg = final_g.reshape(1, d)
    return pl.pallas_call(
        functools.partial(_ffn_kernel, n_chunks=n_chunks),
        grid=(bsz, seq // tm),
        in_specs=[tok(d), pl.BlockSpec((1, y_gelu.shape[1], tm, LANES), lambda b, t: (b, 0, t, 0)),
                  tok(d), tok(d), tok(d), vec(d), vec(d), vec(d), vec(d),
                  const(wglu), const(bglu), const(wout), const(n2), const(wg), const(wu), const(wd), const(fg)],
        out_specs=tok(d),
        out_shape=jax.ShapeDtypeStruct((bsz, seq, d), F32),
        compiler_params=pltpu.CompilerParams(vmem_limit_bytes=VMEM_LIMIT),
        name="merge_ffn",
    )(x, y_gelu, y_att, ga, gb, gate1, shift2, scale2, gate2, wglu, bglu, wout, n2, wg, wu, wd, fg)


def kernel(x, c, w_mod, b_mod, norm1_g, w_in, ssm_a_re, ssm_a_im, ssm_log_dt, ssm_b_re, ssm_b_im, ssm_c_re, ssm_c_im, ssm_d, w_ssm_glu, b_ssm_glu, kv_norm_g, idx_k_norm_g, w_uk, w_uv, w_attn_proj, w_out, norm2_g, w_ffn_gate, w_ffn_up, w_ffn_down, final_g):
    bsz, seq, d = x.shape
    assert w_mod.shape[0] == 1, "single-layer block"
    assert seq % TOKEN_TILE == 0 and seq % KEY_TILE == 0 and TOKEN_TILE % KEY_TILE == 0
    g, p = ssm_a_re.shape[1:]
    cg = ssm_b_re.shape[3]
    assert cg == SSM_GROUP and g % 2 == 0
    n_xs = g * cg
    n_kv, n_heads, d_head = w_uk.shape[1:]
    n_q = n_heads * d_head
    n_ki = idx_k_norm_g.shape[1]
    n_wi = w_in.shape[2] - (n_xs + n_q + n_kv + n_ki + 2 * d)
    n_wi = n_wi // (n_ki + 1)
    n_qi = n_wi * n_ki
    topk = min(TOPK_MAX, seq // 4)

    mod = _modulation(c, w_mod[0], b_mod[0])
    shift1, scale1, gate1, shift2, scale2, gate2 = [m.reshape(bsz, 1, d) for m in jnp.split(mod, 6, axis=-1)]

    offs = {}
    o = 0
    for name, size in (("xs", n_xs), ("q", n_q), ("ckv", n_kv), ("qi", n_qi), ("ki", n_ki), ("wi", n_wi),
                       ("ga", d), ("gb", d)):
        offs[name] = (o, o + size)
        o += size
    assert o == w_in.shape[2]
    wi_all = w_in[0]
    cols = lambda n: wi_all[:, offs[n][0]:offs[n][1]]
    wi_rows = -(-n_wi // 8) * 8
    w_nat = jnp.concatenate([cols("xs"), cols("ga"), cols("gb"), cols("ckv"), cols("ki")], axis=1).astype(BF16)
    w_tr = jnp.concatenate([cols("q"), cols("qi"), cols("ckv"),
                            jnp.pad(cols("wi"), ((0, 0), (0, wi_rows - n_wi)))], axis=1).T.astype(BF16)
    xs, ga, gb, ckv, ki, qT, qiT, ckvT, wiT = _in_projection(
        x, scale1, shift1, norm1_g[0], w_nat, w_tr, kv_norm_g[0], idx_k_norm_g[0],
        (n_xs, d, n_kv, n_ki, n_q, n_qi, wi_rows))

    prep = _ssm_prep(ssm_a_re[0], ssm_a_im[0], ssm_log_dt[0], ssm_b_re[0], ssm_b_im[0], ssm_c_re[0], ssm_c_im[0])
    y_gelu = _s5_branch(xs, _ssm_matrices(*prep, ssm_d[0]))

    y_att = _dsa_branch(qT, qiT, wiT, ki, ckv, ckvT, w_uk[0], w_uv[0], w_attn_proj[0], topk)

    return _merge_ffn(x, y_gelu, y_att, ga, gb, gate1, shift2, scale2, gate2,
                      w_ssm_glu[0], b_ssm_glu[0], w_out[0], norm2_g[0],
                      w_ffn_gate[0], w_ffn_up[0], w_ffn_down[0], final_g)
```

```python
import functools

import jax
import jax.numpy as jnp
from jax import lax
from jax.experimental import pallas as pl
from jax.experimental.pallas import tpu as pltpu

F32 = jnp.float32
BF16 = jnp.bfloat16
HIGHEST = lax.Precision.HIGHEST

EPS = 1e-6
SSM_GROUP = 16
SSM_CHUNK = 16
TOPK_MAX = 256
LANES = 128
Q_BLOCK = 128
KEY_TILE = 256
ATT_STEP_TILES = 1
TOKEN_TILE = 512
ONES_ROWS = 8
INT_MIN = -(2 ** 31)
MASK_VALUE = -2.0 ** 100
NEG_INIT = -2.0 ** 99
LOG2E = 1.4426950408889634
VMEM_LIMIT = 56 * 1024 * 1024


def _rms(x, axis):
    return x * lax.rsqrt(jnp.mean(x * x, axis=axis, keepdims=True) + EPS)


def _mod_kernel(c_ref, w_ref, b_ref, o_ref):
    cv = c_ref[...]
    cond = cv * jax.nn.sigmoid(cv)
    o_ref[...] = jnp.dot(cond, w_ref[...], precision=HIGHEST, preferred_element_type=F32) + b_ref[...]


def _modulation(c, w_mod, b_mod):
    bsz, d = c.shape
    n = w_mod.shape[1]
    rows = -(-bsz // 8) * 8
    cp = jnp.pad(c, ((0, rows - bsz), (0, 0)))
    tn = d
    out = pl.pallas_call(
        _mod_kernel,
        grid=(n // tn,),
        in_specs=[pl.BlockSpec((rows, d), lambda j: (0, 0)),
                  pl.BlockSpec((d, tn), lambda j: (0, j)),
                  pl.BlockSpec((1, tn), lambda j: (0, j))],
        out_specs=pl.BlockSpec((rows, tn), lambda j: (0, j)),
        out_shape=jax.ShapeDtypeStruct((rows, n), F32),
        name="modulation",
    )(cp, w_mod, b_mod.reshape(1, n))
    return out[:bsz]


def _inproj_kernel(x_ref, sc_ref, sh_ref, g1_ref, wn_ref, wt_ref, gkv_ref, gkvc_ref, gki_ref,
                   xs_ref, ga_ref, gb_ref, ckv_ref, ki_ref, qT_ref, qiT_ref, ckvT_ref, wiT_ref,
                   *, n_xs, n_gate, n_kv, n_ki, n_q, n_qi, n_wi):
    x = x_ref[0]
    u = _rms(x, -1) * g1_ref[...]
    u = u * (1.0 + sc_ref[0]) + sh_ref[0]
    ub = u.astype(BF16)
    tm = ub.shape[0]

    zn = jnp.dot(ub, wn_ref[...], preferred_element_type=F32)
    o = 0
    for s in range(n_xs // LANES):
        xs_ref[0, s] = zn[:, s * LANES:(s + 1) * LANES]
    o += n_xs
    ga_ref[0] = zn[:, o:o + n_gate].astype(BF16)
    o += n_gate
    gb_ref[0] = zn[:, o:o + n_gate].astype(BF16)
    o += n_gate
    ckv_ref[0] = (_rms(zn[:, o:o + n_kv], -1) * gkv_ref[...]).astype(BF16)
    o += n_kv
    ki_ref[0] = (_rms(zn[:, o:o + n_ki], -1) * gki_ref[...]).astype(BF16)

    zt = lax.dot_general(wt_ref[...], ub, (((1,), (1,)), ((), ())), preferred_element_type=F32)
    o = 0
    qT_ref[0] = zt[o:o + n_q].astype(BF16)
    o += n_q
    qiT_ref[0] = zt[o:o + n_qi].astype(BF16)
    o += n_qi
    ct = (_rms(zt[o:o + n_kv], 0) * gkvc_ref[...]).astype(BF16)
    o += n_kv
    wiT_ref[0] = zt[o:o + n_wi]
    ones_blk = jnp.where(lax.broadcasted_iota(jnp.int32, (ONES_ROWS, KEY_TILE), 0) == 0, 1.0, 0.0).astype(BF16)
    for i in range(tm // KEY_TILE):
        ckvT_ref[0, i, 0:n_kv, :] = ct[:, i * KEY_TILE:(i + 1) * KEY_TILE]
        ckvT_ref[0, i, n_kv:n_kv + ONES_ROWS, :] = ones_blk


def _in_projection(x, scale1, shift1, norm1_g, w_nat, w_tr, kv_g, ki_g, sizes):
    bsz, seq, d = x.shape
    n_xs, n_gate, n_kv, n_ki, n_q, n_qi, n_wi = sizes
    tm = TOKEN_TILE
    nt = seq // tm
    kern = functools.partial(_inproj_kernel, n_xs=n_xs, n_gate=n_gate, n_kv=n_kv, n_ki=n_ki,
                             n_q=n_q, n_qi=n_qi, n_wi=n_wi)
    tok = lambda w: pl.BlockSpec((1, tm, w), lambda b, t: (b, t, 0))
    lane = lambda r: pl.BlockSpec((1, r, tm), lambda b, t: (b, 0, t))
    vec = lambda w: pl.BlockSpec((1, 1, w), lambda b, t: (b, 0, 0))
    full = lambda a: pl.BlockSpec(a.shape, lambda b, t: (0,) * a.ndim)
    g1 = norm1_g.reshape(1, d)
    gkv = kv_g.reshape(1, n_kv)
    gkvc = kv_g.reshape(n_kv, 1)
    gki = ki_g.reshape(1, n_ki)
    out_shape = (
        jax.ShapeDtypeStruct((bsz, n_xs // LANES, seq, LANES), F32),
        jax.ShapeDtypeStruct((bsz, seq, n_gate), BF16),
        jax.ShapeDtypeStruct((bsz, seq, n_gate), BF16),
        jax.ShapeDtypeStruct((bsz, seq, n_kv), BF16),
        jax.ShapeDtypeStruct((bsz, seq, n_ki), BF16),
        jax.ShapeDtypeStruct((bsz, n_q, seq), BF16),
        jax.ShapeDtypeStruct((bsz, n_qi, seq), BF16),
        jax.ShapeDtypeStruct((bsz, seq // KEY_TILE, n_kv + ONES_ROWS, KEY_TILE), BF16),
        jax.ShapeDtypeStruct((bsz, n_wi, seq), F32),
    )
    out_specs = (
        pl.BlockSpec((1, n_xs // LANES, tm, LANES), lambda b, t: (b, 0, t, 0)),
        tok(n_gate), tok(n_gate), tok(n_kv), tok(n_ki),
        lane(n_q), lane(n_qi),
        pl.BlockSpec((1, tm // KEY_TILE, n_kv + ONES_ROWS, KEY_TILE), lambda b, t: (b, t, 0, 0)),
        lane(n_wi),
    )
    return pl.pallas_call(
        kern,
        grid=(bsz, nt),
        in_specs=[tok(d), vec(d), vec(d), full(g1), full(w_nat), full(w_tr), full(gkv), full(gkvc), full(gki)],
        out_specs=out_specs,
        out_shape=out_shape,
        compiler_params=pltpu.CompilerParams(vmem_limit_bytes=VMEM_LIMIT),
        name="in_projection",
    )(x, scale1, shift1, g1, w_nat, w_tr, gkv, gkvc, gki)


def _ssm_prep_kernel(are_ref, aim_ref, ldt_ref, bre_ref, bim_ref, cre_ref, cim_ref, ctre_ref, ctim_ref,
                     kall_ref, dre_ref, dim_ref, vre_ref, vimn_ref, lre_ref, lim_ref):
    ar = are_ref[0]
    ai = aim_ref[0]
    dt = jnp.exp(ldt_ref[0])
    ard = ar * dt
    aid = ai * dt

    def lam_pow(kf):
        mag = jnp.exp(ard * kf)
        ang = aid * kf
        return mag * jnp.cos(ang), mag * jnp.sin(ang)

    lb_re, lb_im = lam_pow(1.0)
    den = ar * ar + ai * ai
    coef_re = ((lb_re - 1.0) * ar + lb_im * ai) / den
    coef_im = (lb_im * ar - (lb_re - 1.0) * ai) / den
    bre = bre_ref[0]
    bim = bim_ref[0]
    bb_re = coef_re * bre - coef_im * bim
    bb_im = coef_re * bim + coef_im * bre

    width = bre.shape[1]
    lag = lax.shift_right_logical(lax.broadcasted_iota(jnp.int32, (1, width), 1), 4).astype(F32)
    lk_re, lk_im = lam_pow(lag)
    d_re = lk_re * bb_re - lk_im * bb_im
    d_im = lk_re * bb_im + lk_im * bb_re
    dre_ref[0] = d_re
    dim_ref[0] = d_im
    kall_ref[0] = (jnp.dot(cre_ref[0], d_re, precision=HIGHEST, preferred_element_type=F32)
                   - jnp.dot(cim_ref[0], d_im, precision=HIGHEST, preferred_element_type=F32))
    l1_re, l1_im = lam_pow(lag + 1.0)
    ctre = ctre_ref[0]
    ctim = ctim_ref[0]
    vre_ref[0] = l1_re * ctre - l1_im * ctim
    vimn_ref[0] = -(l1_re * ctim + l1_im * ctre)
    lt_re, lt_im = lam_pow(float(SSM_CHUNK))
    lre_ref[0] = lt_re
    lim_ref[0] = lt_im


def _ssm_prep(a_re, a_im, log_dt, b_re, b_im, c_re, c_im):
    g, p = a_re.shape
    cg = b_re.shape[2]
    width = SSM_CHUNK * cg
    col = lambda a: a.reshape(g, p, 1)
    b_t = lambda b: jnp.tile(b, (1, 1, SSM_CHUNK))
    c_t = lambda cc: jnp.tile(jnp.swapaxes(cc, 1, 2), (1, 1, SSM_CHUNK))
    blk = lambda s: pl.BlockSpec((1,) + s, lambda i: (i, 0, 0))
    outs = pl.pallas_call(
        _ssm_prep_kernel,
        grid=(g,),
        in_specs=[blk((p, 1)), blk((p, 1)), blk((1, 1)), blk((p, width)), blk((p, width)),
                  blk((cg, p)), blk((cg, p)), blk((p, width)), blk((p, width))],
        out_specs=(blk((cg, width)), blk((p, width)), blk((p, width)), blk((p, width)), blk((p, width)),
                   blk((p, 1)), blk((p, 1))),
        out_shape=(jax.ShapeDtypeStruct((g, cg, width), F32),) + (jax.ShapeDtypeStruct((g, p, width), F32),) * 4
        + (jax.ShapeDtypeStruct((g, p, 1), F32),) * 2,
        name="ssm_prep",
    )(col(a_re), col(a_im), log_dt.reshape(g, 1, 1), b_t(b_re), b_t(b_im), c_re, c_im, c_t(c_re), c_t(c_im))
    return outs


def _ssm_matrices(kall, d_re, d_im, v_re, v_imn, l_re, l_im, d_skip):
    g, cg, width = kall.shape
    p = d_re.shape[1]
    t = SSM_CHUNK
    gs = LANES // cg
    ns = g // gs
    k4 = kall.reshape(g, cg, t, cg)
    tt = jnp.arange(t)[:, None]
    jj = jnp.arange(t)[None, :]
    lagidx = jnp.clip(tt - jj, 0, t - 1)
    m = k4[:, :, lagidx, :]
    m = jnp.where((tt >= jj)[None, None, :, :, None], m, 0.0)
    m = jnp.transpose(m, (0, 2, 1, 3, 4)).reshape(ns, gs, width, width).astype(BF16)
    w_of = lambda d: d.reshape(g, p, t, cg)[:, :, ::-1, :].reshape(g, p, width)
    w = jnp.concatenate([w_of(d_re), w_of(d_im)], axis=1).reshape(ns, gs, 2 * p, width).astype(BF16)
    v = jnp.concatenate([jnp.swapaxes(v_re, 1, 2), jnp.swapaxes(v_imn, 1, 2)], axis=2)
    v = v.reshape(ns, gs, width, 2 * p).astype(BF16)
    lam = jnp.concatenate([l_re.reshape(ns, 1, gs * p), l_im.reshape(ns, 1, gs * p)], axis=1)
    dsk = jnp.broadcast_to(d_skip.reshape(g, 1, cg), (g, t, cg)).reshape(ns, gs, width, 1)
    return m, w, v, lam, dsk


def _gelu_tanh(y):
    return 0.5 * y * (1.0 + jnp.tanh(0.7978845608028654 * (y + 0.044715 * (y * y * y))))


def _ssm_kernel(xs_ref, m_ref, w_ref, v_ref, lam_ref, d_ref, y_ref,
                zt_ref, ure_ref, uim_ref, sre_ref, sim_ref, yt_ref):
    t = SSM_CHUNK
    cg = SSM_GROUP
    gs = zt_ref.shape[0]
    nch = zt_ref.shape[2]
    p = w_ref.shape[2] // 2

    for ti in range(t):
        xt = xs_ref[0, 0, pl.ds(ti, nch, stride=t), :].T
        for g in range(gs):
            zt_ref[g, ti * cg:(ti + 1) * cg, :] = xt[g * cg:(g + 1) * cg, :].astype(BF16)

    ut = [jnp.dot(w_ref[0, g], zt_ref[g], preferred_element_type=F32) for g in range(gs)]
    ure_ref[...] = jnp.concatenate([u[:p] for u in ut], axis=0).T
    uim_ref[...] = jnp.concatenate([u[p:] for u in ut], axis=0).T

    lr = lam_ref[0, 0:1, :]
    li = lam_ref[0, 1:2, :]

    def scan_body(i, carry):
        s_re, s_im = carry
        sre_ref[pl.ds(i, 1), :] = s_re
        sim_ref[pl.ds(i, 1), :] = s_im
        u_re = ure_ref[pl.ds(i, 1), :]
        u_im = uim_ref[pl.ds(i, 1), :]
        return (lr * s_re - li * s_im + u_re, lr * s_im + li * s_re + u_im)

    zero = jnp.zeros_like(lr)
    lax.fori_loop(0, nch, scan_body, (zero, zero))

    sre_t = sre_ref[...].T
    sim_t = sim_ref[...].T
    for g in range(gs):
        z = zt_ref[g]
        s = jnp.concatenate([sre_t[g * p:(g + 1) * p], sim_t[g * p:(g + 1) * p]], axis=0).astype(BF16)
        y = (jnp.dot(m_ref[0, g], z, preferred_element_type=F32)
             + jnp.dot(v_ref[0, g], s, preferred_element_type=F32)
             + d_ref[0, g] * z.astype(F32))
        yt_ref[g] = _gelu_tanh(y)

    for ti in range(t):
        blk = jnp.concatenate([yt_ref[g, ti * cg:(ti + 1) * cg, :] for g in range(gs)], axis=0)
        y_ref[0, 0, pl.ds(ti, nch, stride=t), :] = blk.T


def _s5_branch(xs, mats):
    m, w, v, lam, dsk = mats
    bsz, ns, seq, lanes = xs.shape
    gs = m.shape[1]
    width = m.shape[2]
    p2 = w.shape[2]
    nch = seq // SSM_CHUNK
    slab = pl.BlockSpec((1, 1, seq, lanes), lambda b, s: (b, s, 0, 0))
    per_slab = lambda a: pl.BlockSpec((1,) + a.shape[1:], lambda b, s: (s,) + (0,) * (a.ndim - 1))
    return pl.pallas_call(
        _ssm_kernel,
        grid=(bsz, ns),
        in_specs=[slab, per_slab(m), per_slab(w), per_slab(v), per_slab(lam), per_slab(dsk)],
        out_specs=slab,
        out_shape=jax.ShapeDtypeStruct(xs.shape, F32),
        scratch_shapes=[pltpu.VMEM((gs, width, nch), BF16),
                        pltpu.VMEM((nch, gs * p2 // 2), F32), pltpu.VMEM((nch, gs * p2 // 2), F32),
                        pltpu.VMEM((nch, gs * p2 // 2), F32), pltpu.VMEM((nch, gs * p2 // 2), F32),
                        pltpu.VMEM((gs, width, nch), F32)],
        compiler_params=pltpu.CompilerParams(vmem_limit_bytes=VMEM_LIMIT),
        name="ssm",
    )(xs, m, w, v, lam, dsk)


def _i32(v):
    return jnp.int32(v - (1 << 32) if v >= (1 << 31) else v)


def _order_pattern(x):
    bits = pltpu.bitcast(x, jnp.int32)
    return bits ^ (lax.shift_right_arithmetic(bits, 31) | jnp.int32(INT_MIN))


def _pattern_to_float(u):
    bits = jnp.where(u < 0, u & jnp.int32(0x7FFFFFFF), ~u)
    return pltpu.bitcast(bits, F32)


def _finite_threshold(u):
    return _pattern_to_float(jnp.where(u < 0, u, jnp.maximum(u, jnp.int32(0x00800000))))


def _bit_transpose32(words):
    a = list(words)
    j, m = 16, 0x0000FFFF
    while j:
        k = 0
        while k < 32:
            t = (a[k] ^ lax.shift_right_logical(a[k + j], j)) & _i32(m)
            a[k] = a[k] ^ t
            a[k + j] = a[k + j] ^ lax.shift_left(t, j)
            k = (k + j + 1) & ~j
        j >>= 1
        m = (m ^ (m << j)) & 0xFFFFFFFF
    return a


def _dsa_kernel(qT_ref, qiT_ref, wiT_ref, ki_ref, ckv_ref, ckvT_ref, wuk_ref, wuvT_ref, wproj_ref,
                o_ref, scores_ref, planes_ref, cand_ref, sel_ref, mask_ref, wq_ref, acc_ref, m_ref,
                sa_ref, sb_ref,
                *, topk, n_heads, d_head, n_iheads, d_ihead, n_kv):
    qb = Q_BLOCK
    tk = KEY_TILE
    words = tk // 8
    j = pl.program_id(1)
    n_tiles = (j * qb + qb + tk - 1) // tk

    for h in range(n_heads):
        qh = qT_ref[0, h * d_head:(h + 1) * d_head, :]
        ql = jnp.dot(wuk_ref[h], qh, preferred_element_type=F32) * (d_head ** -0.5 * LOG2E)
        wq_ref[0:n_kv, h * qb:(h + 1) * qb] = ql.astype(BF16)
    qi_cat = jnp.concatenate([qiT_ref[0, h * d_ihead:(h + 1) * d_ihead, :] for h in range(n_iheads)], axis=1)
    w_heads = wiT_ref[0] * (n_iheads ** -0.5 * d_ihead ** -0.5)

    q_pos = j * qb + lax.broadcasted_iota(jnp.int32, (tk, qb), 1)
    row_iota = lax.broadcasted_iota(jnp.int32, (tk, qb), 0)

    def score_tile(kt, diagonal):
        k0 = pl.multiple_of(kt * tk, tk)
        rel = jnp.dot(ki_ref[0, pl.ds(k0, tk), :], qi_cat, preferred_element_type=F32)
        sc = None
        for h in range(n_iheads):
            term = jnp.maximum(rel[:, h * qb:(h + 1) * qb], 0.0) * w_heads[h:h + 1, :]
            sc = term if sc is None else sc + term
        if diagonal:
            sc = jnp.where((k0 + row_iota) <= q_pos, sc, -jnp.inf)
        scores_ref[kt] = sc
        pat = _order_pattern(sc)
        planes = _bit_transpose32([pat[8 * v:8 * v + 8] for v in range(words)])
        r0 = pl.multiple_of(kt * 8, 8)
        for p in range(32):
            planes_ref[p, pl.ds(r0, 8), :] = planes[p]

    @pl.when(jnp.logical_and(pl.program_id(0) == 0, j == 0))
    def _():
        planes_ref[...] = jnp.zeros(planes_ref.shape, jnp.int32)

    def score_body(i, carry):
        score_tile(2 * i, False)
        score_tile(2 * i + 1, False)
        return carry

    last_pair = (n_tiles + 1) // 2 - 1
    lax.fori_loop(0, last_pair, score_body, 0)
    score_tile(2 * last_pair, True)
    score_tile(2 * last_pair + 1, True)

    def radix_select(rows):
        active = lax.broadcasted_iota(jnp.int32, (rows, qb), 0) < n_tiles * 8
        cand_ref[0:rows, :] = jnp.where(active, jnp.int32(-1), jnp.int32(0))

        def radix_body(p, carry):
            k_rem, pat = carry
            plane = planes_ref[p, 0:rows, :]
            cand = cand_ref[0:rows, :]
            cnt = jnp.sum(lax.population_count(cand & plane), axis=0, keepdims=True)
            take = cnt >= k_rem
            cand_ref[0:rows, :] = cand & (plane ^ jnp.where(take, jnp.int32(0), jnp.int32(-1)))
            bit = lax.shift_left(jnp.int32(1), 31 - p)
            return jnp.where(take, k_rem, k_rem - cnt), jnp.where(take, pat | bit, pat)

        _, pat = lax.fori_loop(0, 32, radix_body,
                               (jnp.full((1, qb), topk, jnp.int32), jnp.zeros((1, qb), jnp.int32)))
        sel_ref[3:4, :] = pltpu.bitcast(pat, F32)

    quarter = cand_ref.shape[0] // 4
    for v in range(1, 5):
        @pl.when(jnp.logical_and(n_tiles * 8 > (v - 1) * quarter, n_tiles * 8 <= v * quarter))
        def _(v=v):
            radix_select(v * quarter)

    t_u = pltpu.bitcast(sel_ref[3:4, :], jnp.int32)

    tile_pairs = (n_tiles + 1) // 2

    def publish(thr):
        def body(i, carry):
            n_gt, n_ge = carry
            for kt in (2 * i, 2 * i + 1):
                x = scores_ref[kt]
                ge = x >= thr
                mask_ref[kt] = jnp.where(ge, 0.0, MASK_VALUE).astype(BF16)
                n_gt = n_gt + jnp.sum(jnp.where(x > thr, 1.0, 0.0), axis=0, keepdims=True)
                n_ge = n_ge + jnp.sum(jnp.where(ge, 1.0, 0.0), axis=0, keepdims=True)
            return n_gt, n_ge
        zero = jnp.zeros((1, qb), F32)
        n_gt, n_ge = lax.fori_loop(0, tile_pairs, body, (zero, zero))
        sel_ref[0:1, :] = thr
        sel_ref[1:2, :] = n_gt
        sel_ref[2:3, :] = n_ge
        return n_gt, n_ge

    n_gt, n_ge = publish(_finite_threshold(t_u))
    few = (j * qb + lax.broadcasted_iota(jnp.int32, (1, qb), 1)) < topk - 1
    good = jnp.logical_or(few, jnp.logical_and(n_gt < topk, n_ge >= topk))
    all_good = jnp.min(jnp.where(good, 1, 0))

    @pl.when(all_good == 0)
    def _():
        def bit_body(i, pat_u):
            cand_u = pat_u | lax.shift_left(jnp.int32(1), 31 - i)
            cand_f = _pattern_to_float(cand_u)

            def cnt_body(kt, acc):
                return acc + jnp.sum(jnp.where(scores_ref[kt] >= cand_f, 1.0, 0.0), axis=0, keepdims=True)

            cnt = lax.fori_loop(0, n_tiles, cnt_body, jnp.zeros((1, qb), F32))
            return jnp.where(cnt >= topk, cand_u, pat_u)

        pat_u = lax.fori_loop(0, 32, bit_body, jnp.zeros((1, qb), jnp.int32))
        publish(_finite_threshold(pat_u))

    thr = sel_ref[0:1, :]
    need = topk - sel_ref[1:2, :]
    most_ge = jnp.max(sel_ref[2:3, :])

    @pl.when(most_ge > topk)
    def _():
        lower = (lax.broadcasted_iota(jnp.int32, (tk, tk), 0) >= lax.broadcasted_iota(jnp.int32, (tk, tk), 1))
        lower = jnp.where(lower, 1.0, 0.0).astype(BF16)

        def bias_body(kt, seen):
            x = scores_ref[kt]
            eq = x == thr
            eqf = jnp.where(eq, 1.0, 0.0).astype(BF16)
            rank = jnp.dot(lower, eqf, preferred_element_type=F32) + seen
            keep_tie = jnp.where(rank <= need, 0.0, MASK_VALUE)
            mask_ref[kt] = jnp.where(x > thr, 0.0, jnp.where(eq, keep_tie, MASK_VALUE)).astype(BF16)
            return rank[tk - 1:tk, :]

        lax.fori_loop(0, n_tiles, bias_body, jnp.zeros((1, qb), F32))

    m_ref[...] = jnp.full(m_ref.shape, NEG_INIT, F32)
    acc_ref[...] = jnp.zeros(acc_ref.shape, F32)
    empty_tile = 2 * tile_pairs
    mask_ref[empty_tile] = jnp.full((tk, qb), MASK_VALUE, BF16)
    max_tile = ckvT_ref.shape[1] - 1
    col_q = lax.broadcasted_iota(jnp.int32, (qb, n_heads * qb), 1) & (qb - 1)
    row_q = lax.broadcasted_iota(jnp.int32, (qb, n_heads * qb), 0)
    wq_ref[n_kv:n_kv + qb, :] = jnp.where(col_q == row_q, 1.0, 0.0).astype(BF16)

    pair = 2 * qb
    n_pairs = n_heads // 2
    spt = ATT_STEP_TILES
    n_steps = (n_tiles + spt - 1) // spt

    def key_rows(kt):
        k0 = pl.multiple_of(jnp.minimum(kt, max_tile) * tk, tk)
        return jnp.concatenate([ckv_ref[0, pl.ds(k0, tk), :], mask_ref[jnp.minimum(kt, empty_tile)]], axis=1)

    def produce(st, s_ref):
        lhs = jnp.concatenate([key_rows(spt * st + r) for r in range(spt)], axis=0)
        for hp in range(n_pairs):
            sl = slice(hp * pair, (hp + 1) * pair)
            s_ref[:, sl] = jnp.dot(lhs, wq_ref[:, sl], preferred_element_type=F32)

    def consume(st, s_ref):
        cv = jnp.concatenate([ckvT_ref[0, jnp.minimum(spt * st + r, max_tile)] for r in range(spt)], axis=1)
        for hp in range(n_pairs):
            sl = slice(hp * pair, (hp + 1) * pair)
            m_old = m_ref[:, sl]
            m_new = jnp.maximum(m_old, jnp.max(s_ref[:, sl], axis=0, keepdims=True))
            p = jnp.exp2(s_ref[:, sl] - m_new).astype(BF16)
            acc_ref[:, sl] = acc_ref[:, sl] * jnp.exp2(m_old - m_new) + jnp.dot(
                cv, p, preferred_element_type=F32)
            m_ref[:, sl] = m_new

    produce(0, sa_ref)

    def step_pair(i, carry):
        produce(2 * i + 1, sb_ref)
        consume(2 * i, sa_ref)
        produce(2 * i + 2, sa_ref)
        consume(2 * i + 1, sb_ref)
        return carry

    lax.fori_loop(0, n_steps // 2, step_pair, 0)

    @pl.when(n_steps % 2 == 1)
    def _():
        consume(n_steps - 1, sa_ref)

    inv_l = 1.0 / acc_ref[n_kv:n_kv + 1, :]
    parts = []
    for h in range(n_heads):
        o_h = (acc_ref[0:n_kv, h * qb:(h + 1) * qb] * inv_l[:, h * qb:(h + 1) * qb]).astype(BF16)
        parts.append(jnp.dot(wuvT_ref[h], o_h, preferred_element_type=F32))
    t_tok = jnp.concatenate(parts, axis=0).T.astype(BF16)
    o_ref[0] = jnp.dot(t_tok, wproj_ref[...], preferred_element_type=F32).astype(BF16)


def _dsa_branch(qT, qiT, wiT, ki, ckv, ckvT, w_uk, w_uv, w_proj, topk):
    bsz, n_q, seq = qT.shape
    n_kv, n_heads, d_head = w_uk.shape
    n_ki = ki.shape[2]
    n_iheads = qiT.shape[1] // n_ki
    d_out = w_proj.shape[1]
    qb, tk = Q_BLOCK, KEY_TILE
    wuk = jnp.transpose(w_uk, (1, 0, 2)).astype(BF16)
    wuvT = jnp.transpose(w_uv, (1, 2, 0)).astype(BF16)
    wproj = w_proj.astype(BF16)
    kern = functools.partial(_dsa_kernel, topk=topk, n_heads=n_heads, d_head=d_head,
                             n_iheads=n_iheads, d_ihead=n_ki, n_kv=n_kv)
    lane = lambda r: pl.BlockSpec((1, r, qb), lambda b, j: (b, 0, j))
    per_b = lambda a: pl.BlockSpec((1,) + a.shape[1:], lambda b, j: (b,) + (0,) * (a.ndim - 1))
    full = lambda a: pl.BlockSpec(a.shape, lambda b, j: (0,) * a.ndim)
    return pl.pallas_call(
        kern,
        grid=(bsz, seq // qb),
        in_specs=[lane(n_q), lane(qiT.shape[1]), lane(wiT.shape[1]), per_b(ki), per_b(ckv), per_b(ckvT),
                  full(wuk), full(wuvT), full(wproj)],
        out_specs=pl.BlockSpec((1, qb, d_out), lambda b, j: (b, j, 0)),
        out_shape=jax.ShapeDtypeStruct((bsz, seq, d_out), BF16),
        scratch_shapes=[pltpu.VMEM((seq // tk, tk, qb), F32),
                        pltpu.VMEM((32, seq // tk * 8, qb), jnp.int32),
                        pltpu.VMEM((seq // tk * 8, qb), jnp.int32),
                        pltpu.VMEM((8, qb), F32),
                        pltpu.VMEM((seq // tk + 1, tk, qb), BF16),
                        pltpu.VMEM((n_kv + qb, n_heads * qb), BF16),
                        pltpu.VMEM((n_kv + ONES_ROWS, n_heads * qb), F32),
                        pltpu.VMEM((1, n_heads * qb), F32),
                        pltpu.VMEM((ATT_STEP_TILES * tk, n_heads * qb), F32),
                        pltpu.VMEM((ATT_STEP_TILES * tk, n_heads * qb), F32)],
        compiler_params=pltpu.CompilerParams(vmem_limit_bytes=VMEM_LIMIT),
        name="dsa",
    )(qT, qiT, wiT, ki, ckv, ckvT, wuk, wuvT, wproj)


def _ffn_kernel(x_ref, y_ref, a_ref, ga_ref, gb_ref, g1_ref, sh2_ref, sc2_ref, g2_ref,
                wglu_ref, bglu_ref, wout_ref, n2_ref, wg_ref, wu_ref, wd_ref, fg_ref, o_ref, *, n_chunks):
    x = x_ref[0]
    y_gelu = jnp.concatenate([y_ref[0, s] for s in range(y_ref.shape[1])], axis=1).astype(BF16)
    z = jnp.dot(y_gelu, wglu_ref[...], preferred_element_type=F32) + bglu_ref[...]
    d = z.shape[1] // 2
    y_ssm = z[:, :d] * jax.nn.sigmoid(z[:, d:])
    merged = (jax.nn.sigmoid(ga_ref[0].astype(F32)) * y_ssm
              + jax.nn.sigmoid(gb_ref[0].astype(F32)) * a_ref[0].astype(F32))
    h1 = x + g1_ref[0] * jnp.dot(merged.astype(BF16), wout_ref[...], preferred_element_type=F32)
    u2 = _rms(h1, -1) * n2_ref[...]
    u2 = (u2 * (1.0 + sc2_ref[0]) + sh2_ref[0]).astype(BF16)
    hidden = wg_ref.shape[1]
    step = hidden // n_chunks
    ffn = None
    for ci in range(n_chunks):
        sl = slice(ci * step, (ci + 1) * step)
        gate = jnp.dot(u2, wg_ref[:, sl], preferred_element_type=F32)
        up = jnp.dot(u2, wu_ref[:, sl], preferred_element_type=F32)
        act = ((gate * jax.nn.sigmoid(gate)) * up).astype(BF16)
        part = jnp.dot(act, wd_ref[sl, :], preferred_element_type=F32)
        ffn = part if ffn is None else ffn + part
    h2 = h1 + g2_ref[0] * ffn
    o_ref[0] = _rms(h2, -1) * fg_ref[...]


def _merge_ffn(x, y_gelu, y_att, ga, gb, gate1, shift2, scale2, gate2,
               w_glu, b_glu, w_out, norm2_g, w_gate, w_up, w_down, final_g):
    bsz, seq, d = x.shape
    tm = TOKEN_TILE
    hidden = w_gate.shape[1]
    n_chunks = 2 if hidden % 256 == 0 else 1
    tok = lambda w: pl.BlockSpec((1, tm, w), lambda b, t: (b, t, 0))
    vec = lambda w: pl.BlockSpec((1, 1, w), lambda b, t: (b, 0, 0))
    const = lambda a: pl.BlockSpec(a.shape, lambda b, t: (0,) * a.ndim, pipeline_mode=pl.Buffered(1))
    wglu = w_glu.astype(BF16)
    bglu = b_glu.reshape(1, -1)
    wout = w_out.astype(BF16)
    n2 = norm2_g.reshape(1, d)
    wg = w_gate.astype(BF16)
    wu = w_up.astype(BF16)
    wd = w_down.astype(BF16)
    fg = final_g.reshape(1, d)
    return pl.pallas_call(
        functools.partial(_ffn_kernel, n_chunks=n_chunks),
        grid=(bsz, seq // tm),
        in_specs=[tok(d), pl.BlockSpec((1, y_gelu.shape[1], tm, LANES), lambda b, t: (b, 0, t, 0)),
                  tok(d), tok(d), tok(d), vec(d), vec(d), vec(d), vec(d),
                  const(wglu), const(bglu), const(wout), const(n2), const(wg), const(wu), const(wd), const(fg)],
        out_specs=tok(d),
        out_shape=jax.ShapeDtypeStruct((bsz, seq, d), F32),
        compiler_params=pltpu.CompilerParams(vmem_limit_bytes=VMEM_LIMIT),
        name="merge_ffn",
    )(x, y_gelu, y_att, ga, gb, gate1, shift2, scale2, gate2, wglu, bglu, wout, n2, wg, wu, wd, fg)


def kernel(x, c, w_mod, b_mod, norm1_g, w_in, ssm_a_re, ssm_a_im, ssm_log_dt, ssm_b_re, ssm_b_im, ssm_c_re, ssm_c_im, ssm_d, w_ssm_glu, b_ssm_glu, kv_norm_g, idx_k_norm_g, w_uk, w_uv, w_attn_proj, w_out, norm2_g, w_ffn_gate, w_ffn_up, w_ffn_down, final_g):
    bsz, seq, d = x.shape
    assert w_mod.shape[0] == 1, "single-layer block"
    assert seq % TOKEN_TILE == 0 and seq % (4 * KEY_TILE) == 0 and TOKEN_TILE % KEY_TILE == 0
    g, p = ssm_a_re.shape[1:]
    cg = ssm_b_re.shape[3]
    assert cg == SSM_GROUP and g % 2 == 0
    n_xs = g * cg
    n_kv, n_heads, d_head = w_uk.shape[1:]
    n_q = n_heads * d_head
    n_ki = idx_k_norm_g.shape[1]
    n_wi = w_in.shape[2] - (n_xs + n_q + n_kv + n_ki + 2 * d)
    n_wi = n_wi // (n_ki + 1)
    n_qi = n_wi * n_ki
    topk = min(TOPK_MAX, seq // 4)

    mod = _modulation(c, w_mod[0], b_mod[0])
    shift1, scale1, gate1, shift2, scale2, gate2 = [m.reshape(bsz, 1, d) for m in jnp.split(mod, 6, axis=-1)]

    offs = {}
    o = 0
    for name, size in (("xs", n_xs), ("q", n_q), ("ckv", n_kv), ("qi", n_qi), ("ki", n_ki), ("wi", n_wi),
                       ("ga", d), ("gb", d)):
        offs[name] = (o, o + size)
        o += size
    assert o == w_in.shape[2]
    wi_all = w_in[0]
    cols = lambda n: wi_all[:, offs[n][0]:offs[n][1]]
    wi_rows = -(-n_wi // 8) * 8
    w_nat = jnp.concatenate([cols("xs"), cols("ga"), cols("gb"), cols("ckv"), cols("ki")], axis=1).astype(BF16)
    w_tr = jnp.concatenate([cols("q"), cols("qi"), cols("ckv"),
                            jnp.pad(cols("wi"), ((0, 0), (0, wi_rows - n_wi)))], axis=1).T.astype(BF16)
    xs, ga, gb, ckv, ki, qT, qiT, ckvT, wiT = _in_projection(
        x, scale1, shift1, norm1_g[0], w_nat, w_tr, kv_norm_g[0], idx_k_norm_g[0],
        (n_xs, d, n_kv, n_ki, n_q, n_qi, wi_rows))

    prep = _ssm_prep(ssm_a_re[0], ssm_a_im[0], ssm_log_dt[0], ssm_b_re[0], ssm_b_im[0], ssm_c_re[0], ssm_c_im[0])
    y_gelu = _s5_branch(xs, _ssm_matrices(*prep, ssm_d[0]))

    y_att = _dsa_branch(qT, qiT, wiT, ki, ckv, ckvT, w_uk[0], w_uv[0], w_attn_proj[0], topk)

    return _merge_ffn(x, y_gelu, y_att, ga, gb, gate1, shift2, scale2, gate2,
                      w_ssm_glu[0], b_ssm_glu[0], w_out[0], norm2_g[0],
                      w_ffn_gate[0], w_ffn_up[0], w_ffn_down[0], final_g)
```

```python
import functools

import jax
import jax.numpy as jnp
from jax import lax
from jax.experimental import pallas as pl
from jax.experimental.pallas import tpu as pltpu

F32 = jnp.float32
BF16 = jnp.bfloat16
HIGHEST = lax.Precision.HIGHEST

EPS = 1e-6
SSM_GROUP = 16
SSM_CHUNK = 16
TOPK_MAX = 256
LANES = 128
Q_BLOCK = 128
KEY_TILE = 256
ATT_STEP_TILES = 1
TOKEN_TILE = 512
ONES_ROWS = 8
INT_MIN = -(2 ** 31)
MASK_VALUE = -2.0 ** 100
NEG_INIT = -2.0 ** 99
LOG2E = 1.4426950408889634
VMEM_LIMIT = 56 * 1024 * 1024


def _rms(x, axis):
    return x * lax.rsqrt(jnp.mean(x * x, axis=axis, keepdims=True) + EPS)


def _mod_kernel(c_ref, w_ref, b_ref, o_ref):
    cv = c_ref[...]
    cond = cv * jax.nn.sigmoid(cv)
    o_ref[...] = jnp.dot(cond, w_ref[...], precision=HIGHEST, preferred_element_type=F32) + b_ref[...]


def _modulation(c, w_mod, b_mod):
    bsz, d = c.shape
    n = w_mod.shape[1]
    rows = -(-bsz // 8) * 8
    cp = jnp.pad(c, ((0, rows - bsz), (0, 0)))
    tn = d
    out = pl.pallas_call(
        _mod_kernel,
        grid=(n // tn,),
        in_specs=[pl.BlockSpec((rows, d), lambda j: (0, 0)),
                  pl.BlockSpec((d, tn), lambda j: (0, j)),
                  pl.BlockSpec((1, tn), lambda j: (0, j))],
        out_specs=pl.BlockSpec((rows, tn), lambda j: (0, j)),
        out_shape=jax.ShapeDtypeStruct((rows, n), F32),
        name="modulation",
    )(cp, w_mod, b_mod.reshape(1, n))
    return out[:bsz]


def _inproj_kernel(x_ref, sc_ref, sh_ref, g1_ref, wn_ref, wt_ref, gkv_ref, gkvc_ref, gki_ref,
                   xs_ref, ga_ref, gb_ref, ckv_ref, ki_ref, qT_ref, qiT_ref, ckvT_ref, wiT_ref,
                   *, n_xs, n_gate, n_kv, n_ki, n_q, n_qi, n_wi):
    x = x_ref[0]
    u = _rms(x, -1) * g1_ref[...]
    u = u * (1.0 + sc_ref[0]) + sh_ref[0]
    ub = u.astype(BF16)
    tm = ub.shape[0]

    zn = jnp.dot(ub, wn_ref[...], preferred_element_type=F32)
    o = 0
    for s in range(n_xs // LANES):
        xs_ref[0, s] = zn[:, s * LANES:(s + 1) * LANES]
    o += n_xs
    ga_ref[0] = zn[:, o:o + n_gate].astype(BF16)
    o += n_gate
    gb_ref[0] = zn[:, o:o + n_gate].astype(BF16)
    o += n_gate
    ckv_ref[0] = (_rms(zn[:, o:o + n_kv], -1) * gkv_ref[...]).astype(BF16)
    o += n_kv
    ki_ref[0] = (_rms(zn[:, o:o + n_ki], -1) * gki_ref[...]).astype(BF16)

    zt = lax.dot_general(wt_ref[...], ub, (((1,), (1,)), ((), ())), preferred_element_type=F32)
    o = 0
    qT_ref[0] = zt[o:o + n_q].astype(BF16)
    o += n_q
    qiT_ref[0] = zt[o:o + n_qi].astype(BF16)
    o += n_qi
    ct = (_rms(zt[o:o + n_kv], 0) * gkvc_ref[...]).astype(BF16)
    o += n_kv
    wiT_ref[0] = zt[o:o + n_wi]
    ones_blk = jnp.where(lax.broadcasted_iota(jnp.int32, (ONES_ROWS, KEY_TILE), 0) == 0, 1.0, 0.0).astype(BF16)
    for i in range(tm // KEY_TILE):
        ckvT_ref[0, i, 0:n_kv, :] = ct[:, i * KEY_TILE:(i + 1) * KEY_TILE]
        ckvT_ref[0, i, n_kv:n_kv + ONES_ROWS, :] = ones_blk


def _in_projection(x, scale1, shift1, norm1_g, w_nat, w_tr, kv_g, ki_g, sizes):
    bsz, seq, d = x.shape
    n_xs, n_gate, n_kv, n_ki, n_q, n_qi, n_wi = sizes
    tm = TOKEN_TILE
    nt = seq // tm
    kern = functools.partial(_inproj_kernel, n_xs=n_xs, n_gate=n_gate, n_kv=n_kv, n_ki=n_ki,
                             n_q=n_q, n_qi=n_qi, n_wi=n_wi)
    tok = lambda w: pl.BlockSpec((1, tm, w), lambda b, t: (b, t, 0))
    lane = lambda r: pl.BlockSpec((1, r, tm), lambda b, t: (b, 0, t))
    vec = lambda w: pl.BlockSpec((1, 1, w), lambda b, t: (b, 0, 0))
    full = lambda a: pl.BlockSpec(a.shape, lambda b, t: (0,) * a.ndim)
    g1 = norm1_g.reshape(1, d)
    gkv = kv_g.reshape(1, n_kv)
    gkvc = kv_g.reshape(n_kv, 1)
    gki = ki_g.reshape(1, n_ki)
    out_shape = (
        jax.ShapeDtypeStruct((bsz, n_xs // LANES, seq, LANES), F32),
        jax.ShapeDtypeStruct((bsz, seq, n_gate), BF16),
        jax.ShapeDtypeStruct((bsz, seq, n_gate), BF16),
        jax.ShapeDtypeStruct((bsz, seq, n_kv), BF16),
        jax.ShapeDtypeStruct((bsz, seq, n_ki), BF16),
        jax.ShapeDtypeStruct((bsz, n_q, seq), BF16),
        jax.ShapeDtypeStruct((bsz, n_qi, seq), BF16),
        jax.ShapeDtypeStruct((bsz, seq // KEY_TILE, n_kv + ONES_ROWS, KEY_TILE), BF16),
        jax.ShapeDtypeStruct((bsz, n_wi, seq), F32),
    )
    out_specs = (
        pl.BlockSpec((1, n_xs // LANES, tm, LANES), lambda b, t: (b, 0, t, 0)),
        tok(n_gate), tok(n_gate), tok(n_kv), tok(n_ki),
        lane(n_q), lane(n_qi),
        pl.BlockSpec((1, tm // KEY_TILE, n_kv + ONES_ROWS, KEY_TILE), lambda b, t: (b, t, 0, 0)),
        lane(n_wi),
    )
    return pl.pallas_call(
        kern,
        grid=(bsz, nt),
        in_specs=[tok(d), vec(d), vec(d), full(g1), full(w_nat), full(w_tr), full(gkv), full(gkvc), full(gki)],
        out_specs=out_specs,
        out_shape=out_shape,
        compiler_params=pltpu.CompilerParams(vmem_limit_bytes=VMEM_LIMIT),
        name="in_projection",
    )(x, scale1, shift1, g1, w_nat, w_tr, gkv, gkvc, gki)


def _ssm_prep_kernel(are_ref, aim_ref, ldt_ref, bre_ref, bim_ref, cre_ref, cim_ref, ctre_ref, ctim_ref,
                     kall_ref, dre_ref, dim_ref, vre_ref, vimn_ref, lre_ref, lim_ref):
    for gi in range(are_ref.shape[0]):
        _ssm_prep_group(gi, are_ref, aim_ref, ldt_ref, bre_ref, bim_ref, cre_ref, cim_ref, ctre_ref, ctim_ref,
                        kall_ref, dre_ref, dim_ref, vre_ref, vimn_ref, lre_ref, lim_ref)


def _ssm_prep_group(gi, are_ref, aim_ref, ldt_ref, bre_ref, bim_ref, cre_ref, cim_ref, ctre_ref, ctim_ref,
                    kall_ref, dre_ref, dim_ref, vre_ref, vimn_ref, lre_ref, lim_ref):
    ar = are_ref[gi]
    ai = aim_ref[gi]
    dt = jnp.exp(ldt_ref[gi])
    ard = ar * dt
    aid = ai * dt

    def lam_pow(kf):
        mag = jnp.exp(ard * kf)
        ang = aid * kf
        return mag * jnp.cos(ang), mag * jnp.sin(ang)

    lb_re, lb_im = lam_pow(1.0)
    den = ar * ar + ai * ai
    coef_re = ((lb_re - 1.0) * ar + lb_im * ai) / den
    coef_im = (lb_im * ar - (lb_re - 1.0) * ai) / den
    bre = bre_ref[gi]
    bim = bim_ref[gi]
    bb_re = coef_re * bre - coef_im * bim
    bb_im = coef_re * bim + coef_im * bre

    width = bre.shape[1]
    lag = lax.shift_right_logical(lax.broadcasted_iota(jnp.int32, (1, width), 1), 4).astype(F32)
    lk_re, lk_im = lam_pow(lag)
    d_re = lk_re * bb_re - lk_im * bb_im
    d_im = lk_re * bb_im + lk_im * bb_re
    dre_ref[gi] = d_re
    dim_ref[gi] = d_im
    kall_ref[gi] = (jnp.dot(cre_ref[gi], d_re, precision=HIGHEST, preferred_element_type=F32)
                    - jnp.dot(cim_ref[gi], d_im, precision=HIGHEST, preferred_element_type=F32))
    l1_re, l1_im = lam_pow(lag + 1.0)
    ctre = ctre_ref[gi]
    ctim = ctim_ref[gi]
    vre_ref[gi] = l1_re * ctre - l1_im * ctim
    vimn_ref[gi] = -(l1_re * ctim + l1_im * ctre)
    lt_re, lt_im = lam_pow(float(SSM_CHUNK))
    lre_ref[gi] = lt_re
    lim_ref[gi] = lt_im


def _ssm_prep(a_re, a_im, log_dt, b_re, b_im, c_re, c_im):
    g, p = a_re.shape
    cg = b_re.shape[2]
    width = SSM_CHUNK * cg
    col = lambda a: a.reshape(g, p, 1)
    b_t = lambda b: jnp.tile(b, (1, 1, SSM_CHUNK))
    c_t = lambda cc: jnp.tile(jnp.swapaxes(cc, 1, 2), (1, 1, SSM_CHUNK))
    gpb = LANES // cg
    blk = lambda s: pl.BlockSpec((gpb,) + s, lambda i: (i, 0, 0))
    outs = pl.pallas_call(
        _ssm_prep_kernel,
        grid=(g // gpb,),
        in_specs=[blk((p, 1)), blk((p, 1)), blk((1, 1)), blk((p, width)), blk((p, width)),
                  blk((cg, p)), blk((cg, p)), blk((p, width)), blk((p, width))],
        out_specs=(blk((cg, width)), blk((p, width)), blk((p, width)), blk((p, width)), blk((p, width)),
                   blk((p, 1)), blk((p, 1))),
        out_shape=(jax.ShapeDtypeStruct((g, cg, width), F32),) + (jax.ShapeDtypeStruct((g, p, width), F32),) * 4
        + (jax.ShapeDtypeStruct((g, p, 1), F32),) * 2,
        name="ssm_prep",
    )(col(a_re), col(a_im), log_dt.reshape(g, 1, 1), b_t(b_re), b_t(b_im), c_re, c_im, c_t(c_re), c_t(c_im))
    return outs


def _ssm_matrices(kall, d_re, d_im, v_re, v_imn, l_re, l_im, d_skip):
    g, cg, width = kall.shape
    p = d_re.shape[1]
    t = SSM_CHUNK
    gs = LANES // cg
    ns = g // gs
    kflip = kall.reshape(g, cg, t, cg)[:, :, ::-1, :].reshape(g, cg, width)
    kz = jnp.pad(kflip, ((0, 0), (0, 0), (0, width)))
    m = jnp.stack([kz[:, :, cg * (t - 1 - ti):cg * (t - 1 - ti) + width] for ti in range(t)], axis=1)
    m = m.reshape(ns, gs, width, width).astype(BF16)
    w_of = lambda d: d.reshape(g, p, t, cg)[:, :, ::-1, :].reshape(g, p, width)
    w = jnp.concatenate([w_of(d_re), w_of(d_im)], axis=1).reshape(ns, gs, 2 * p, width).astype(BF16)
    v = jnp.concatenate([jnp.swapaxes(v_re, 1, 2), jnp.swapaxes(v_imn, 1, 2)], axis=2)
    v = v.reshape(ns, gs, width, 2 * p).astype(BF16)
    lam = jnp.concatenate([l_re.reshape(ns, 1, gs * p), l_im.reshape(ns, 1, gs * p)], axis=1)
    dsk = jnp.broadcast_to(d_skip.reshape(g, 1, cg), (g, t, cg)).reshape(ns, gs, width, 1)
    return m, w, v, lam, dsk


def _gelu_tanh(y):
    return 0.5 * y * (1.0 + jnp.tanh(0.7978845608028654 * (y + 0.044715 * (y * y * y))))


def _ssm_kernel(xs_ref, m_ref, w_ref, v_ref, lam_ref, d_ref, y_ref,
                zt_ref, ure_ref, uim_ref, sre_ref, sim_ref, yt_ref):
    t = SSM_CHUNK
    cg = SSM_GROUP
    gs = zt_ref.shape[0]
    nch = zt_ref.shape[2]
    p = w_ref.shape[2] // 2

    for ti in range(t):
        xt = xs_ref[0, 0, pl.ds(ti, nch, stride=t), :].T
        for g in range(gs):
            zt_ref[g, ti * cg:(ti + 1) * cg, :] = xt[g * cg:(g + 1) * cg, :].astype(BF16)

    ut = [jnp.dot(w_ref[0, g], zt_ref[g], preferred_element_type=F32) for g in range(gs)]
    ure_ref[...] = jnp.concatenate([u[:p] for u in ut], axis=0).T
    uim_ref[...] = jnp.concatenate([u[p:] for u in ut], axis=0).T

    lr = lam_ref[0, 0:1, :]
    li = lam_ref[0, 1:2, :]

    def scan_body(i, carry):
        s_re, s_im = carry
        sre_ref[pl.ds(i, 1), :] = s_re
        sim_ref[pl.ds(i, 1), :] = s_im
        u_re = ure_ref[pl.ds(i, 1), :]
        u_im = uim_ref[pl.ds(i, 1), :]
        return (lr * s_re - li * s_im + u_re, lr * s_im + li * s_re + u_im)

    zero = jnp.zeros_like(lr)
    lax.fori_loop(0, nch, scan_body, (zero, zero))

    sre_t = sre_ref[...].T
    sim_t = sim_ref[...].T
    for g in range(gs):
        z = zt_ref[g]
        s = jnp.concatenate([sre_t[g * p:(g + 1) * p], sim_t[g * p:(g + 1) * p]], axis=0).astype(BF16)
        y = (jnp.dot(m_ref[0, g], z, preferred_element_type=F32)
             + jnp.dot(v_ref[0, g], s, preferred_element_type=F32)
             + d_ref[0, g] * z.astype(F32))
        yt_ref[g] = _gelu_tanh(y)

    for ti in range(t):
        blk = jnp.concatenate([yt_ref[g, ti * cg:(ti + 1) * cg, :] for g in range(gs)], axis=0)
        y_ref[0, 0, pl.ds(ti, nch, stride=t), :] = blk.T


def _s5_branch(xs, mats):
    m, w, v, lam, dsk = mats
    bsz, ns, seq, lanes = xs.shape
    gs = m.shape[1]
    width = m.shape[2]
    p2 = w.shape[2]
    nch = seq // SSM_CHUNK
    slab = pl.BlockSpec((1, 1, seq, lanes), lambda b, s: (b, s, 0, 0))
    per_slab = lambda a: pl.BlockSpec((1,) + a.shape[1:], lambda b, s: (s,) + (0,) * (a.ndim - 1))
    return pl.pallas_call(
        _ssm_kernel,
        grid=(bsz, ns),
        in_specs=[slab, per_slab(m), per_slab(w), per_slab(v), per_slab(lam), per_slab(dsk)],
        out_specs=slab,
        out_shape=jax.ShapeDtypeStruct(xs.shape, F32),
        scratch_shapes=[pltpu.VMEM((gs, width, nch), BF16),
                        pltpu.VMEM((nch, gs * p2 // 2), F32), pltpu.VMEM((nch, gs * p2 // 2), F32),
                        pltpu.VMEM((nch, gs * p2 // 2), F32), pltpu.VMEM((nch, gs * p2 // 2), F32),
                        pltpu.VMEM((gs, width, nch), F32)],
        compiler_params=pltpu.CompilerParams(vmem_limit_bytes=VMEM_LIMIT),
        name="ssm",
    )(xs, m, w, v, lam, dsk)


def _i32(v):
    return jnp.int32(v - (1 << 32) if v >= (1 << 31) else v)


def _order_pattern(x):
    bits = pltpu.bitcast(x, jnp.int32)
    return bits ^ (lax.shift_right_arithmetic(bits, 31) | jnp.int32(INT_MIN))


def _pattern_to_float(u):
    bits = jnp.where(u < 0, u & jnp.int32(0x7FFFFFFF), ~u)
    return pltpu.bitcast(bits, F32)


def _finite_threshold(u):
    return _pattern_to_float(jnp.where(u < 0, u, jnp.maximum(u, jnp.int32(0x00800000))))


def _bit_transpose32(words):
    a = list(words)
    j, m = 16, 0x0000FFFF
    while j:
        k = 0
        while k < 32:
            t = (a[k] ^ lax.shift_right_logical(a[k + j], j)) & _i32(m)
            a[k] = a[k] ^ t
            a[k + j] = a[k + j] ^ lax.shift_left(t, j)
            k = (k + j + 1) & ~j
        j >>= 1
        m = (m ^ (m << j)) & 0xFFFFFFFF
    return a


def _dsa_kernel(qT_ref, qiT_ref, wiT_ref, ki_ref, ckv_ref, ckvT_ref, wuk_ref, wuvT_ref, wproj_ref,
                o_ref, scores_ref, planes_ref, cand_ref, sel_ref, mask_ref, wq_ref, acc_ref, m_ref,
                sa_ref, sb_ref,
                *, topk, n_heads, d_head, n_iheads, d_ihead, n_kv):
    qb = Q_BLOCK
    tk = KEY_TILE
    words = tk // 8
    j = pl.program_id(1)
    n_tiles = (j * qb + qb + tk - 1) // tk

    for h in range(n_heads):
        qh = qT_ref[0, h * d_head:(h + 1) * d_head, :]
        ql = jnp.dot(wuk_ref[h], qh, preferred_element_type=F32) * (d_head ** -0.5 * LOG2E)
        wq_ref[0:n_kv, h * qb:(h + 1) * qb] = ql.astype(BF16)
    qi_cat = jnp.concatenate([qiT_ref[0, h * d_ihead:(h + 1) * d_ihead, :] for h in range(n_iheads)], axis=1)
    w_heads = wiT_ref[0] * (n_iheads ** -0.5 * d_ihead ** -0.5)

    q_pos = j * qb + lax.broadcasted_iota(jnp.int32, (tk, qb), 1)
    row_iota = lax.broadcasted_iota(jnp.int32, (tk, qb), 0)

    def score_tile(kt, diagonal):
        k0 = pl.multiple_of(kt * tk, tk)
        rel = jnp.dot(ki_ref[0, pl.ds(k0, tk), :], qi_cat, preferred_element_type=F32)
        sc = None
        for h in range(n_iheads):
            term = jnp.maximum(rel[:, h * qb:(h + 1) * qb], 0.0) * w_heads[h:h + 1, :]
            sc = term if sc is None else sc + term
        if diagonal:
            sc = jnp.where((k0 + row_iota) <= q_pos, sc, -jnp.inf)
        scores_ref[kt] = sc
        pat = _order_pattern(sc)
        planes = _bit_transpose32([pat[8 * v:8 * v + 8] for v in range(words)])
        r0 = pl.multiple_of(kt * 8, 8)
        for p in range(32):
            planes_ref[p, pl.ds(r0, 8), :] = planes[p]

    @pl.when(jnp.logical_and(pl.program_id(0) == 0, j == 0))
    def _():
        planes_ref[...] = jnp.zeros(planes_ref.shape, jnp.int32)

    def score_body(i, carry):
        score_tile(2 * i, False)
        score_tile(2 * i + 1, False)
        return carry

    last_pair = (n_tiles + 1) // 2 - 1
    lax.fori_loop(0, last_pair, score_body, 0)
    score_tile(2 * last_pair, True)
    score_tile(2 * last_pair + 1, True)

    def radix_select(rows):
        active = lax.broadcasted_iota(jnp.int32, (rows, qb), 0) < n_tiles * 8
        cand_ref[0:rows, :] = jnp.where(active, jnp.int32(-1), jnp.int32(0))

        def radix_body(p, carry):
            k_rem, pat = carry
            plane = planes_ref[p, 0:rows, :]
            cand = cand_ref[0:rows, :]
            cnt = jnp.sum(lax.population_count(cand & plane), axis=0, keepdims=True)
            take = cnt >= k_rem
            cand_ref[0:rows, :] = cand & (plane ^ jnp.where(take, jnp.int32(0), jnp.int32(-1)))
            bit = lax.shift_left(jnp.int32(1), 31 - p)
            return jnp.where(take, k_rem, k_rem - cnt), jnp.where(take, pat | bit, pat)

        _, pat = lax.fori_loop(0, 32, radix_body,
                               (jnp.full((1, qb), topk, jnp.int32), jnp.zeros((1, qb), jnp.int32)))
        sel_ref[3:4, :] = pltpu.bitcast(pat, F32)

    quarter = cand_ref.shape[0] // 4
    for v in range(1, 5):
        @pl.when(jnp.logical_and(n_tiles * 8 > (v - 1) * quarter, n_tiles * 8 <= v * quarter))
        def _(v=v):
            radix_select(v * quarter)

    t_u = pltpu.bitcast(sel_ref[3:4, :], jnp.int32)

    tile_pairs = (n_tiles + 1) // 2

    def publish(thr):
        def body(i, carry):
            n_gt, n_ge = carry
            for kt in (2 * i, 2 * i + 1):
                x = scores_ref[kt]
                ge = x >= thr
                mask_ref[kt] = jnp.where(ge, 0.0, MASK_VALUE).astype(BF16)
                n_gt = n_gt + jnp.sum(jnp.where(x > thr, 1.0, 0.0), axis=0, keepdims=True)
                n_ge = n_ge + jnp.sum(jnp.where(ge, 1.0, 0.0), axis=0, keepdims=True)
            return n_gt, n_ge
        zero = jnp.zeros((1, qb), F32)
        n_gt, n_ge = lax.fori_loop(0, tile_pairs, body, (zero, zero))
        sel_ref[0:1, :] = thr
        sel_ref[1:2, :] = n_gt
        sel_ref[2:3, :] = n_ge
        return n_gt, n_ge

    n_gt, n_ge = publish(_finite_threshold(t_u))
    few = (j * qb + lax.broadcasted_iota(jnp.int32, (1, qb), 1)) < topk - 1
    good = jnp.logical_or(few, jnp.logical_and(n_gt < topk, n_ge >= topk))
    all_good = jnp.min(jnp.where(good, 1, 0))

    @pl.when(all_good == 0)
    def _():
        def bit_body(i, pat_u):
            cand_u = pat_u | lax.shift_left(jnp.int32(1), 31 - i)
            cand_f = _pattern_to_float(cand_u)

            def cnt_body(kt, acc):
                return acc + jnp.sum(jnp.where(scores_ref[kt] >= cand_f, 1.0, 0.0), axis=0, keepdims=True)

            cnt = lax.fori_loop(0, n_tiles, cnt_body, jnp.zeros((1, qb), F32))
            return jnp.where(cnt >= topk, cand_u, pat_u)

        pat_u = lax.fori_loop(0, 32, bit_body, jnp.zeros((1, qb), jnp.int32))
        publish(_finite_threshold(pat_u))

    thr = sel_ref[0:1, :]
    need = topk - sel_ref[1:2, :]
    most_ge = jnp.max(sel_ref[2:3, :])

    @pl.when(most_ge > topk)
    def _():
        lower = (lax.broadcasted_iota(jnp.int32, (tk, tk), 0) >= lax.broadcasted_iota(jnp.int32, (tk, tk), 1))
        lower = jnp.where(lower, 1.0, 0.0).astype(BF16)

        def bias_body(kt, seen):
            x = scores_ref[kt]
            eq = x == thr
            eqf = jnp.where(eq, 1.0, 0.0).astype(BF16)
            rank = jnp.dot(lower, eqf, preferred_element_type=F32) + seen
            keep_tie = jnp.where(rank <= need, 0.0, MASK_VALUE)
            mask_ref[kt] = jnp.where(x > thr, 0.0, jnp.where(eq, keep_tie, MASK_VALUE)).astype(BF16)
            return rank[tk - 1:tk, :]

        lax.fori_loop(0, n_tiles, bias_body, jnp.zeros((1, qb), F32))

    m_ref[...] = jnp.full(m_ref.shape, NEG_INIT, F32)
    acc_ref[...] = jnp.zeros(acc_ref.shape, F32)
    empty_tile = 2 * tile_pairs
    mask_ref[empty_tile] = jnp.full((tk, qb), MASK_VALUE, BF16)
    max_tile = ckvT_ref.shape[1] - 1
    col_q = lax.broadcasted_iota(jnp.int32, (qb, n_heads * qb), 1) & (qb - 1)
    row_q = lax.broadcasted_iota(jnp.int32, (qb, n_heads * qb), 0)
    wq_ref[n_kv:n_kv + qb, :] = jnp.where(col_q == row_q, 1.0, 0.0).astype(BF16)

    pair = 2 * qb
    n_pairs = n_heads // 2
    spt = ATT_STEP_TILES
    n_steps = (n_tiles + spt - 1) // spt

    def key_rows(kt):
        k0 = pl.multiple_of(jnp.minimum(kt, max_tile) * tk, tk)
        return jnp.concatenate([ckv_ref[0, pl.ds(k0, tk), :], mask_ref[jnp.minimum(kt, empty_tile)]], axis=1)

    def produce(st, s_ref):
        lhs = jnp.concatenate([key_rows(spt * st + r) for r in range(spt)], axis=0)
        for hp in range(n_pairs):
            sl = slice(hp * pair, (hp + 1) * pair)
            s_ref[:, sl] = jnp.dot(lhs, wq_ref[:, sl], preferred_element_type=F32)

    def consume(st, s_ref):
        cv = jnp.concatenate([ckvT_ref[0, jnp.minimum(spt * st + r, max_tile)] for r in range(spt)], axis=1)
        for hp in range(n_pairs):
            sl = slice(hp * pair, (hp + 1) * pair)
            m_old = m_ref[:, sl]
            m_new = jnp.maximum(m_old, jnp.max(s_ref[:, sl], axis=0, keepdims=True))
            p = jnp.exp2(s_ref[:, sl] - m_new).astype(BF16)
            acc_ref[:, sl] = acc_ref[:, sl] * jnp.exp2(m_old - m_new) + jnp.dot(
                cv, p, preferred_element_type=F32)
            m_ref[:, sl] = m_new

    produce(0, sa_ref)

    def two_steps(base):
        produce(base + 1, sb_ref)
        consume(base, sa_ref)
        produce(base + 2, sa_ref)
        consume(base + 1, sb_ref)

    def four_steps(i, carry):
        two_steps(4 * i)
        two_steps(4 * i + 2)
        return carry

    lax.fori_loop(0, n_steps // 4, four_steps, 0)
    done = (n_steps // 4) * 4

    @pl.when(n_steps - done >= 2)
    def _():
        two_steps(done)

    @pl.when(n_steps % 2 == 1)
    def _():
        consume(n_steps - 1, sa_ref)

    inv_l = 1.0 / acc_ref[n_kv:n_kv + 1, :]
    parts = []
    for h in range(n_heads):
        o_h = (acc_ref[0:n_kv, h * qb:(h + 1) * qb] * inv_l[:, h * qb:(h + 1) * qb]).astype(BF16)
        parts.append(jnp.dot(wuvT_ref[h], o_h, preferred_element_type=F32))
    t_tok = jnp.concatenate(parts, axis=0).T.astype(BF16)
    o_ref[0] = jnp.dot(t_tok, wproj_ref[...], preferred_element_type=F32).astype(BF16)


def _dsa_branch(qT, qiT, wiT, ki, ckv, ckvT, w_uk, w_uv, w_proj, topk):
    bsz, n_q, seq = qT.shape
    n_kv, n_heads, d_head = w_uk.shape
    n_ki = ki.shape[2]
    n_iheads = qiT.shape[1] // n_ki
    d_out = w_proj.shape[1]
    qb, tk = Q_BLOCK, KEY_TILE
    wuk = jnp.transpose(w_uk, (1, 0, 2)).astype(BF16)
    wuvT = jnp.transpose(w_uv, (1, 2, 0)).astype(BF16)
    wproj = w_proj.astype(BF16)
    kern = functools.partial(_dsa_kernel, topk=topk, n_heads=n_heads, d_head=d_head,
                             n_iheads=n_iheads, d_ihead=n_ki, n_kv=n_kv)
    lane = lambda r: pl.BlockSpec((1, r, qb), lambda b, j: (b, 0, j))
    per_b = lambda a: pl.BlockSpec((1,) + a.shape[1:], lambda b, j: (b,) + (0,) * (a.ndim - 1))
    full = lambda a: pl.BlockSpec(a.shape, lambda b, j: (0,) * a.ndim)
    return pl.pallas_call(
        kern,
        grid=(bsz, seq // qb),
        in_specs=[lane(n_q), lane(qiT.shape[1]), lane(wiT.shape[1]), per_b(ki), per_b(ckv), per_b(ckvT),
                  full(wuk), full(wuvT), full(wproj)],
        out_specs=pl.BlockSpec((1, qb, d_out), lambda b, j: (b, j, 0)),
        out_shape=jax.ShapeDtypeStruct((bsz, seq, d_out), BF16),
        scratch_shapes=[pltpu.VMEM((seq // tk, tk, qb), F32),
                        pltpu.VMEM((32, seq // tk * 8, qb), jnp.int32),
                        pltpu.VMEM((seq // tk * 8, qb), jnp.int32),
                        pltpu.VMEM((8, qb), F32),
                        pltpu.VMEM((seq // tk + 1, tk, qb), BF16),
                        pltpu.VMEM((n_kv + qb, n_heads * qb), BF16),
                        pltpu.VMEM((n_kv + ONES_ROWS, n_heads * qb), F32),
                        pltpu.VMEM((1, n_heads * qb), F32),
                        pltpu.VMEM((ATT_STEP_TILES * tk, n_heads * qb), F32),
                        pltpu.VMEM((ATT_STEP_TILES * tk, n_heads * qb), F32)],
        compiler_params=pltpu.CompilerParams(vmem_limit_bytes=VMEM_LIMIT),
        name="dsa",
    )(qT, qiT, wiT, ki, ckv, ckvT, wuk, wuvT, wproj)


def _ffn_kernel(x_ref, y_ref, a_ref, ga_ref, gb_ref, g1_ref, sh2_ref, sc2_ref, g2_ref,
                wglu_ref, bglu_ref, wout_ref, n2_ref, wg_ref, wu_ref, wd_ref, fg_ref, o_ref, *, n_chunks):
    x = x_ref[0]
    y_gelu = jnp.concatenate([y_ref[0, s] for s in range(y_ref.shape[1])], axis=1).astype(BF16)
    z = jnp.dot(y_gelu, wglu_ref[...], preferred_element_type=F32) + bglu_ref[...]
    d = z.shape[1] // 2
    y_ssm = z[:, :d] * jax.nn.sigmoid(z[:, d:])
    merged = (jax.nn.sigmoid(ga_ref[0].astype(F32)) * y_ssm
              + jax.nn.sigmoid(gb_ref[0].astype(F32)) * a_ref[0].astype(F32))
    h1 = x + g1_ref[0] * jnp.dot(merged.astype(BF16), wout_ref[...], preferred_element_type=F32)
    u2 = _rms(h1, -1) * n2_ref[...]
    u2 = (u2 * (1.0 + sc2_ref[0]) + sh2_ref[0]).astype(BF16)
    hidden = wg_ref.shape[1]
    step = hidden // n_chunks
    ffn = None
    for ci in range(n_chunks):
        sl = slice(ci * step, (ci + 1) * step)
        gate = jnp.dot(u2, wg_ref[:, sl], preferred_element_type=F32)
        up = jnp.dot(u2, wu_ref[:, sl], preferred_element_type=F32)
        act = ((gate * jax.nn.sigmoid(gate)) * up).astype(BF16)
        part = jnp.dot(act, wd_ref[sl, :], preferred_element_type=F32)
        ffn = part if ffn is None else ffn + part
    h2 = h1 + g2_ref[0] * ffn
    o_ref[0] = _rms(h2, -1) * fg_ref[...]


def _merge_ffn(x, y_gelu, y_att, ga, gb, gate1, shift2, scale2, gate2,
               w_glu, b_glu, w_out, norm2_g, w_gate, w_up, w_down, final_g):
    bsz, seq, d = x.shape
    tm = TOKEN_TILE
    hidden = w_gate.shape[1]
    n_chunks = 2 if hidden % 256 == 0 else 1
    tok = lambda w: pl.BlockSpec((1, tm, w), lambda b, t: (b, t, 0))
    vec = lambda w: pl.BlockSpec((1, 1, w), lambda b, t: (b, 0, 0))
    const = lambda a: pl.BlockSpec(a.shape, lambda b, t: (0,) * a.ndim, pipeline_mode=pl.Buffered(1))
    wglu = w_glu.astype(BF16)
    bglu = b_glu.reshape(1, -1)
    wout = w_out.astype(BF16)
    n2 = norm2_g.reshape(1, d)
    wg = w_gate.astype(BF16)
    wu = w_up.astype(BF16)
    wd = w_down.astype(BF16)
    fg = final_g.reshape(1, d)
    return pl.pallas_call(
        functools.partial(_ffn_kernel, n_chunks=n_chunks),
        grid=(bsz, seq // tm),
        in_specs=[tok(d), pl.BlockSpec((1, y_gelu.shape[1], tm, LANES), lambda b, t: (b, 0, t, 0)),
                  tok(d), tok(d), tok(d), vec(d), vec(d), vec(d), vec(d),
                  const(wglu), const(bglu), const(wout), const(n2), const(wg), const(wu), const(wd), const(fg)],
        out_specs=tok(d),
        out_shape=jax.ShapeDtypeStruct((bsz, seq, d), F32),
        compiler_params=pltpu.CompilerParams(vmem_limit_bytes=VMEM_LIMIT),
        name="merge_ffn",
    )(x, y_gelu, y_att, ga, gb, gate1, shift2, scale2, gate2, wglu, bglu, wout, n2, wg, wu, wd, fg)


def kernel(x, c, w_mod, b_mod, norm1_g, w_in, ssm_a_re, ssm_a_im, ssm_log_dt, ssm_b_re, ssm_b_im, ssm_c_re, ssm_c_im, ssm_d, w_ssm_glu, b_ssm_glu, kv_norm_g, idx_k_norm_g, w_uk, w_uv, w_attn_proj, w_out, norm2_g, w_ffn_gate, w_ffn_up, w_ffn_down, final_g):
    bsz, seq, d = x.shape
    assert w_mod.shape[0] == 1, "single-layer block"
    assert seq % TOKEN_TILE == 0 and seq % (4 * KEY_TILE) == 0 and TOKEN_TILE % KEY_TILE == 0
    g, p = ssm_a_re.shape[1:]
    cg = ssm_b_re.shape[3]
    assert cg == SSM_GROUP and g % 2 == 0
    n_xs = g * cg
    n_kv, n_heads, d_head = w_uk.shape[1:]
    n_q = n_heads * d_head
    n_ki = idx_k_norm_g.shape[1]
    n_wi = w_in.shape[2] - (n_xs + n_q + n_kv + n_ki + 2 * d)
    n_wi = n_wi // (n_ki + 1)
    n_qi = n_wi * n_ki
    topk = min(TOPK_MAX, seq // 4)

    mod = _modulation(c, w_mod[0], b_mod[0])
    shift1, scale1, gate1, shift2, scale2, gate2 = [m.reshape(bsz, 1, d) for m in jnp.split(mod, 6, axis=-1)]

    offs = {}
    o = 0
    for name, size in (("xs", n_xs), ("q", n_q), ("ckv", n_kv), ("qi", n_qi), ("ki", n_ki), ("wi", n_wi),
                       ("ga", d), ("gb", d)):
        offs[name] = (o, o + size)
        o += size
    assert o == w_in.shape[2]
    wi_all = w_in[0]
    cols = lambda n: wi_all[:, offs[n][0]:offs[n][1]]
    wi_rows = -(-n_wi // 8) * 8
    w_nat = jnp.concatenate([cols("xs"), cols("ga"), cols("gb"), cols("ckv"), cols("ki")], axis=1).astype(BF16)
    w_tr = jnp.concatenate([cols("q"), cols("qi"), cols("ckv"),
                            jnp.pad(cols("wi"), ((0, 0), (0, wi_rows - n_wi)))], axis=1).T.astype(BF16)
    xs, ga, gb, ckv, ki, qT, qiT, ckvT, wiT = _in_projection(
        x, scale1, shift1, norm1_g[0], w_nat, w_tr, kv_norm_g[0], idx_k_norm_g[0],
        (n_xs, d, n_kv, n_ki, n_q, n_qi, wi_rows))

    prep = _ssm_prep(ssm_a_re[0], ssm_a_im[0], ssm_log_dt[0], ssm_b_re[0], ssm_b_im[0], ssm_c_re[0], ssm_c_im[0])
    y_gelu = _s5_branch(xs, _ssm_matrices(*prep, ssm_d[0]))

    y_att = _dsa_branch(qT, qiT, wiT, ki, ckv, ckvT, w_uk[0], w_uv[0], w_attn_proj[0], topk)

    return _merge_ffn(x, y_gelu, y_att, ga, gb, gate1, shift2, scale2, gate2,
                      w_ssm_glu[0], b_ssm_glu[0], w_out[0], norm2_g[0],
                      w_ffn_gate[0], w_ffn_up[0], w_ffn_down[0], final_g)
```

```python
import functools

import jax
import jax.numpy as jnp
from jax import lax
from jax.experimental import pallas as pl
from jax.experimental.pallas import tpu as pltpu

F32 = jnp.float32
BF16 = jnp.bfloat16
HIGHEST = lax.Precision.HIGHEST

EPS = 1e-6
SSM_GROUP = 16
SSM_CHUNK = 16
TOPK_MAX = 256
LANES = 128
MXU_TILE = 256
Q_BLOCK = 128
KEY_TILE = 256
ATT_STEP_TILES = 1
TOKEN_TILE = 512
ONES_ROWS = 8
INT_MIN = -(2 ** 31)
MASK_VALUE = -2.0 ** 100
NEG_INIT = -2.0 ** 99
LOG2E = 1.4426950408889634
VMEM_LIMIT = 56 * 1024 * 1024


def _rms(x, axis):
    return x * lax.rsqrt(jnp.mean(x * x, axis=axis, keepdims=True) + EPS)


def _mod_kernel(c_ref, w_ref, b_ref, o_ref):
    cv = c_ref[...]
    cond = cv * jax.nn.sigmoid(cv)
    o_ref[...] = jnp.dot(cond, w_ref[...], precision=HIGHEST, preferred_element_type=F32) + b_ref[...]


def _modulation(c, w_mod, b_mod):
    bsz, d = c.shape
    n = w_mod.shape[1]
    rows = -(-bsz // 8) * 8
    cp = jnp.pad(c, ((0, rows - bsz), (0, 0)))
    tn = d
    out = pl.pallas_call(
        _mod_kernel,
        grid=(n // tn,),
        in_specs=[pl.BlockSpec((rows, d), lambda j: (0, 0)),
                  pl.BlockSpec((d, tn), lambda j: (0, j)),
                  pl.BlockSpec((1, tn), lambda j: (0, j))],
        out_specs=pl.BlockSpec((rows, tn), lambda j: (0, j)),
        out_shape=jax.ShapeDtypeStruct((rows, n), F32),
        name="modulation",
    )(cp, w_mod, b_mod.reshape(1, n))
    return out[:bsz]


def _inproj_kernel(x_ref, sc_ref, sh_ref, g1_ref, wn_ref, wt_ref, gkv_ref, gkvc_ref, gki_ref,
                   xs_ref, ga_ref, gb_ref, ckv_ref, ki_ref, qT_ref, qiT_ref, ckvT_ref, wiT_ref,
                   *, n_xs, n_gate, n_kv, n_ki, n_q, n_qi, n_wi):
    x = x_ref[0]
    u = _rms(x, -1) * g1_ref[...]
    u = u * (1.0 + sc_ref[0]) + sh_ref[0]
    ub = u.astype(BF16)
    tm = ub.shape[0]

    zn = jnp.dot(ub, wn_ref[...], preferred_element_type=F32)
    o = 0
    for s in range(n_xs // LANES):
        xs_ref[0, s] = zn[:, s * LANES:(s + 1) * LANES]
    o += n_xs
    ga_ref[0] = zn[:, o:o + n_gate].astype(BF16)
    o += n_gate
    gb_ref[0] = zn[:, o:o + n_gate].astype(BF16)
    o += n_gate
    ckv_ref[0] = (_rms(zn[:, o:o + n_kv], -1) * gkv_ref[...]).astype(BF16)
    o += n_kv
    ki_ref[0] = (_rms(zn[:, o:o + n_ki], -1) * gki_ref[...]).astype(BF16)

    zt = lax.dot_general(wt_ref[...], ub, (((1,), (1,)), ((), ())), preferred_element_type=F32)
    o = 0
    qT_ref[0] = zt[o:o + n_q].astype(BF16)
    o += n_q
    qiT_ref[0] = zt[o:o + n_qi].astype(BF16)
    o += n_qi
    ct = (_rms(zt[o:o + n_kv], 0) * gkvc_ref[...]).astype(BF16)
    o += n_kv
    wiT_ref[0] = zt[o:o + n_wi]
    ones_blk = jnp.where(lax.broadcasted_iota(jnp.int32, (ONES_ROWS, KEY_TILE), 0) == 0, 1.0, 0.0).astype(BF16)
    for i in range(tm // KEY_TILE):
        ckvT_ref[0, i, 0:n_kv, :] = ct[:, i * KEY_TILE:(i + 1) * KEY_TILE]
        ckvT_ref[0, i, n_kv:n_kv + ONES_ROWS, :] = ones_blk


def _in_projection(x, scale1, shift1, norm1_g, w_nat, w_tr, kv_g, ki_g, sizes):
    bsz, seq, d = x.shape
    n_xs, n_gate, n_kv, n_ki, n_q, n_qi, n_wi = sizes
    tm = TOKEN_TILE
    nt = seq // tm
    kern = functools.partial(_inproj_kernel, n_xs=n_xs, n_gate=n_gate, n_kv=n_kv, n_ki=n_ki,
                             n_q=n_q, n_qi=n_qi, n_wi=n_wi)
    tok = lambda w: pl.BlockSpec((1, tm, w), lambda b, t: (b, t, 0))
    lane = lambda r: pl.BlockSpec((1, r, tm), lambda b, t: (b, 0, t))
    vec = lambda w: pl.BlockSpec((1, 1, w), lambda b, t: (b, 0, 0))
    full = lambda a: pl.BlockSpec(a.shape, lambda b, t: (0,) * a.ndim)
    g1 = norm1_g.reshape(1, d)
    gkv = kv_g.reshape(1, n_kv)
    gkvc = kv_g.reshape(n_kv, 1)
    gki = ki_g.reshape(1, n_ki)
    out_shape = (
        jax.ShapeDtypeStruct((bsz, n_xs // LANES, seq, LANES), F32),
        jax.ShapeDtypeStruct((bsz, seq, n_gate), BF16),
        jax.ShapeDtypeStruct((bsz, seq, n_gate), BF16),
        jax.ShapeDtypeStruct((bsz, seq, n_kv), BF16),
        jax.ShapeDtypeStruct((bsz, seq, n_ki), BF16),
        jax.ShapeDtypeStruct((bsz, n_q, seq), BF16),
        jax.ShapeDtypeStruct((bsz, n_qi, seq), BF16),
        jax.ShapeDtypeStruct((bsz, seq // KEY_TILE, n_kv + ONES_ROWS, KEY_TILE), BF16),
        jax.ShapeDtypeStruct((bsz, n_wi, seq), F32),
    )
    out_specs = (
        pl.BlockSpec((1, n_xs // LANES, tm, LANES), lambda b, t: (b, 0, t, 0)),
        tok(n_gate), tok(n_gate), tok(n_kv), tok(n_ki),
        lane(n_q), lane(n_qi),
        pl.BlockSpec((1, tm // KEY_TILE, n_kv + ONES_ROWS, KEY_TILE), lambda b, t: (b, t, 0, 0)),
        lane(n_wi),
    )
    return pl.pallas_call(
        kern,
        grid=(bsz, nt),
        in_specs=[tok(d), vec(d), vec(d), full(g1), full(w_nat), full(w_tr), full(gkv), full(gkvc), full(gki)],
        out_specs=out_specs,
        out_shape=out_shape,
        compiler_params=pltpu.CompilerParams(vmem_limit_bytes=VMEM_LIMIT),
        name="in_projection",
    )(x, scale1, shift1, g1, w_nat, w_tr, gkv, gkvc, gki)


def _ssm_prep_kernel(are_ref, aim_ref, ldt_ref, bre_ref, bim_ref, cre_ref, cim_ref, ctre_ref, ctim_ref,
                     kall_ref, dre_ref, dim_ref, vre_ref, vimn_ref, lre_ref, lim_ref):
    for gi in range(are_ref.shape[0]):
        _ssm_prep_group(gi, are_ref, aim_ref, ldt_ref, bre_ref, bim_ref, cre_ref, cim_ref, ctre_ref, ctim_ref,
                        kall_ref, dre_ref, dim_ref, vre_ref, vimn_ref, lre_ref, lim_ref)


def _ssm_prep_group(gi, are_ref, aim_ref, ldt_ref, bre_ref, bim_ref, cre_ref, cim_ref, ctre_ref, ctim_ref,
                    kall_ref, dre_ref, dim_ref, vre_ref, vimn_ref, lre_ref, lim_ref):
    ar = are_ref[gi]
    ai = aim_ref[gi]
    dt = jnp.exp(ldt_ref[gi])
    ard = ar * dt
    aid = ai * dt

    def lam_pow(kf):
        mag = jnp.exp(ard * kf)
        ang = aid * kf
        return mag * jnp.cos(ang), mag * jnp.sin(ang)

    lb_re, lb_im = lam_pow(1.0)
    den = ar * ar + ai * ai
    coef_re = ((lb_re - 1.0) * ar + lb_im * ai) / den
    coef_im = (lb_im * ar - (lb_re - 1.0) * ai) / den
    bre = bre_ref[gi]
    bim = bim_ref[gi]
    bb_re = coef_re * bre - coef_im * bim
    bb_im = coef_re * bim + coef_im * bre

    width = bre.shape[1]
    lag = lax.shift_right_logical(lax.broadcasted_iota(jnp.int32, (1, width), 1), 4).astype(F32)
    lk_re, lk_im = lam_pow(lag)
    d_re = lk_re * bb_re - lk_im * bb_im
    d_im = lk_re * bb_im + lk_im * bb_re
    dre_ref[gi] = d_re
    dim_ref[gi] = d_im
    kall_ref[gi] = (jnp.dot(cre_ref[gi], d_re, precision=HIGHEST, preferred_element_type=F32)
                    - jnp.dot(cim_ref[gi], d_im, precision=HIGHEST, preferred_element_type=F32))
    l1_re, l1_im = lam_pow(lag + 1.0)
    ctre = ctre_ref[gi]
    ctim = ctim_ref[gi]
    vre_ref[gi] = l1_re * ctre - l1_im * ctim
    vimn_ref[gi] = -(l1_re * ctim + l1_im * ctre)
    lt_re, lt_im = lam_pow(float(SSM_CHUNK))
    lre_ref[gi] = lt_re
    lim_ref[gi] = lt_im


def _ssm_prep(a_re, a_im, log_dt, b_re, b_im, c_re, c_im):
    g, p = a_re.shape
    cg = b_re.shape[2]
    width = SSM_CHUNK * cg
    col = lambda a: a.reshape(g, p, 1)
    b_t = lambda b: jnp.tile(b, (1, 1, SSM_CHUNK))
    c_t = lambda cc: jnp.tile(jnp.swapaxes(cc, 1, 2), (1, 1, SSM_CHUNK))
    gpb = LANES // cg
    blk = lambda s: pl.BlockSpec((gpb,) + s, lambda i: (i, 0, 0))
    outs = pl.pallas_call(
        _ssm_prep_kernel,
        grid=(g // gpb,),
        in_specs=[blk((p, 1)), blk((p, 1)), blk((1, 1)), blk((p, width)), blk((p, width)),
                  blk((cg, p)), blk((cg, p)), blk((p, width)), blk((p, width))],
        out_specs=(blk((cg, width)), blk((p, width)), blk((p, width)), blk((p, width)), blk((p, width)),
                   blk((p, 1)), blk((p, 1))),
        out_shape=(jax.ShapeDtypeStruct((g, cg, width), F32),) + (jax.ShapeDtypeStruct((g, p, width), F32),) * 4
        + (jax.ShapeDtypeStruct((g, p, 1), F32),) * 2,
        name="ssm_prep",
    )(col(a_re), col(a_im), log_dt.reshape(g, 1, 1), b_t(b_re), b_t(b_im), c_re, c_im, c_t(c_re), c_t(c_im))
    return outs


def _ssm_matrices(kall, d_re, d_im, v_re, v_imn, l_re, l_im, d_skip):
    g, cg, width = kall.shape
    p = d_re.shape[1]
    t = SSM_CHUNK
    gs = LANES // cg
    ns = g // gs
    kflip = kall.reshape(g, cg, t, cg)[:, :, ::-1, :].reshape(g, cg, width)
    kz = jnp.pad(kflip, ((0, 0), (0, 0), (0, width)))
    m = jnp.stack([kz[:, :, cg * (t - 1 - ti):cg * (t - 1 - ti) + width] for ti in range(t)], axis=1)
    m = m.reshape(ns, gs, width, width).astype(BF16)
    w_of = lambda d: d.reshape(g, p, t, cg)[:, :, ::-1, :].reshape(g, p, width)
    w = jnp.concatenate([w_of(d_re), w_of(d_im)], axis=1).reshape(ns, gs, 2 * p, width).astype(BF16)
    v = jnp.concatenate([jnp.swapaxes(v_re, 1, 2), jnp.swapaxes(v_imn, 1, 2)], axis=2)
    v = v.reshape(ns, gs, width, 2 * p).astype(BF16)
    lam = jnp.concatenate([l_re.reshape(ns, 1, gs * p), l_im.reshape(ns, 1, gs * p)], axis=1)
    dsk = jnp.broadcast_to(d_skip.reshape(g, 1, cg), (g, t, cg)).reshape(ns, gs, width, 1)
    return m, w, v, lam, dsk


def _gelu_tanh(y):
    return 0.5 * y * (1.0 + jnp.tanh(0.7978845608028654 * (y + 0.044715 * (y * y * y))))


def _ssm_kernel(xs_ref, m_ref, w_ref, v_ref, lam_ref, d_ref, y_ref,
                zt_ref, ure_ref, uim_ref, sre_ref, sim_ref, yt_ref):
    t = SSM_CHUNK
    cg = SSM_GROUP
    gs = zt_ref.shape[0]
    nch = zt_ref.shape[2]
    p = w_ref.shape[2] // 2

    for ti in range(t):
        xt = xs_ref[0, 0, pl.ds(ti, nch, stride=t), :].T
        for g in range(gs):
            zt_ref[g, ti * cg:(ti + 1) * cg, :] = xt[g * cg:(g + 1) * cg, :].astype(BF16)

    ut = [jnp.dot(w_ref[0, g], zt_ref[g], preferred_element_type=F32) for g in range(gs)]
    ure_ref[...] = jnp.concatenate([u[:p] for u in ut], axis=0).T
    uim_ref[...] = jnp.concatenate([u[p:] for u in ut], axis=0).T

    lr = lam_ref[0, 0:1, :]
    li = lam_ref[0, 1:2, :]

    def scan_body(i, carry):
        s_re, s_im = carry
        sre_ref[pl.ds(i, 1), :] = s_re
        sim_ref[pl.ds(i, 1), :] = s_im
        u_re = ure_ref[pl.ds(i, 1), :]
        u_im = uim_ref[pl.ds(i, 1), :]
        return (lr * s_re - li * s_im + u_re, lr * s_im + li * s_re + u_im)

    zero = jnp.zeros_like(lr)
    lax.fori_loop(0, nch, scan_body, (zero, zero))

    sre_t = sre_ref[...].T
    sim_t = sim_ref[...].T
    for g in range(gs):
        z = zt_ref[g]
        s = jnp.concatenate([sre_t[g * p:(g + 1) * p], sim_t[g * p:(g + 1) * p]], axis=0).astype(BF16)
        y = (jnp.dot(m_ref[0, g], z, preferred_element_type=F32)
             + jnp.dot(v_ref[0, g], s, preferred_element_type=F32)
             + d_ref[0, g] * z.astype(F32))
        yt_ref[g] = _gelu_tanh(y)

    for ti in range(t):
        blk = jnp.concatenate([yt_ref[g, ti * cg:(ti + 1) * cg, :] for g in range(gs)], axis=0)
        y_ref[0, 0, pl.ds(ti, nch, stride=t), :] = blk.T


def _s5_branch(xs, mats):
    m, w, v, lam, dsk = mats
    bsz, ns, seq, lanes = xs.shape
    gs = m.shape[1]
    width = m.shape[2]
    p2 = w.shape[2]
    nch = seq // SSM_CHUNK
    slab = pl.BlockSpec((1, 1, seq, lanes), lambda b, s: (b, s, 0, 0))
    per_slab = lambda a: pl.BlockSpec((1,) + a.shape[1:], lambda b, s: (s,) + (0,) * (a.ndim - 1))
    return pl.pallas_call(
        _ssm_kernel,
        grid=(bsz, ns),
        in_specs=[slab, per_slab(m), per_slab(w), per_slab(v), per_slab(lam), per_slab(dsk)],
        out_specs=slab,
        out_shape=jax.ShapeDtypeStruct(xs.shape, F32),
        scratch_shapes=[pltpu.VMEM((gs, width, nch), BF16),
                        pltpu.VMEM((nch, gs * p2 // 2), F32), pltpu.VMEM((nch, gs * p2 // 2), F32),
                        pltpu.VMEM((nch, gs * p2 // 2), F32), pltpu.VMEM((nch, gs * p2 // 2), F32),
                        pltpu.VMEM((gs, width, nch), F32)],
        compiler_params=pltpu.CompilerParams(vmem_limit_bytes=VMEM_LIMIT),
        name="ssm",
    )(xs, m, w, v, lam, dsk)


def _i32(v):
    return jnp.int32(v - (1 << 32) if v >= (1 << 31) else v)


def _order_pattern(x):
    bits = pltpu.bitcast(x, jnp.int32)
    return bits ^ (lax.shift_right_arithmetic(bits, 31) | jnp.int32(INT_MIN))


def _pattern_to_float(u):
    bits = jnp.where(u < 0, u & jnp.int32(0x7FFFFFFF), ~u)
    return pltpu.bitcast(bits, F32)


def _finite_threshold(u):
    return _pattern_to_float(jnp.where(u < 0, u, jnp.maximum(u, jnp.int32(0x00800000))))


def _bit_transpose32(words):
    a = list(words)
    j, m = 16, 0x0000FFFF
    while j:
        k = 0
        while k < 32:
            t = (a[k] ^ lax.shift_right_logical(a[k + j], j)) & _i32(m)
            a[k] = a[k] ^ t
            a[k + j] = a[k + j] ^ lax.shift_left(t, j)
            k = (k + j + 1) & ~j
        j >>= 1
        m = (m ^ (m << j)) & 0xFFFFFFFF
    return a


def _dsa_kernel(qT_ref, qiT_ref, wiT_ref, ki_ref, ckv_ref, ckvT_ref, wuk_ref, wuvT_ref, wproj_ref,
                o_ref, scores_ref, planes_ref, cand_ref, sel_ref, mask_ref, wq_ref, acc_ref, m_ref,
                sa_ref, sb_ref,
                *, topk, n_heads, d_head, n_iheads, d_ihead, n_kv):
    qb = Q_BLOCK
    tk = KEY_TILE
    words = tk // 8
    j = pl.program_id(1)
    n_tiles = (j * qb + qb + tk - 1) // tk

    for h in range(n_heads):
        qh = qT_ref[0, h * d_head:(h + 1) * d_head, :]
        ql = jnp.dot(wuk_ref[h], qh, preferred_element_type=F32) * (d_head ** -0.5 * LOG2E)
        wq_ref[0:n_kv, h * qb:(h + 1) * qb] = ql.astype(BF16)
    qi_cat = jnp.concatenate([qiT_ref[0, h * d_ihead:(h + 1) * d_ihead, :] for h in range(n_iheads)], axis=1)
    w_heads = wiT_ref[0] * (n_iheads ** -0.5 * d_ihead ** -0.5)

    q_pos = j * qb + lax.broadcasted_iota(jnp.int32, (tk, qb), 1)
    row_iota = lax.broadcasted_iota(jnp.int32, (tk, qb), 0)

    def score_tile(kt, diagonal):
        k0 = pl.multiple_of(kt * tk, tk)
        rel = jnp.dot(ki_ref[0, pl.ds(k0, tk), :], qi_cat, preferred_element_type=F32)
        sc = None
        for h in range(n_iheads):
            term = jnp.maximum(rel[:, h * qb:(h + 1) * qb], 0.0) * w_heads[h:h + 1, :]
            sc = term if sc is None else sc + term
        if diagonal:
            sc = jnp.where((k0 + row_iota) <= q_pos, sc, -jnp.inf)
        scores_ref[kt] = sc
        bits = pltpu.bitcast(sc, jnp.int32)
        planes = _bit_transpose32([bits[8 * v:8 * v + 8] for v in range(words)])
        sign = planes[0]
        planes_ref[kt, 0] = ~sign
        for p in range(1, 32):
            planes_ref[kt, p] = planes[p] ^ sign

    @pl.when(jnp.logical_and(pl.program_id(0) == 0, j == 0))
    def _():
        planes_ref[...] = jnp.zeros(planes_ref.shape, jnp.int32)

    def score_pair(i):
        score_tile(2 * i, False)
        score_tile(2 * i + 1, False)

    def score_body(i, carry):
        score_pair(2 * i)
        score_pair(2 * i + 1)
        return carry

    last_pair = (n_tiles + 1) // 2 - 1
    lax.fori_loop(0, last_pair // 2, score_body, 0)

    @pl.when(last_pair % 2 == 1)
    def _():
        score_pair(last_pair - 1)

    score_tile(2 * last_pair, True)
    score_tile(2 * last_pair + 1, True)

    def radix_select(nt):
        active = lax.broadcasted_iota(jnp.int32, (nt, 8, qb), 0) < n_tiles
        cand_ref[0:nt] = jnp.where(active, jnp.int32(-1), jnp.int32(0))

        def radix_body(p, carry):
            k_rem, pat = carry
            plane = planes_ref[0:nt, p]
            cand = cand_ref[0:nt]
            ones = lax.population_count(cand & plane)
            cnt = jnp.sum(jnp.sum(ones, axis=0), axis=0, keepdims=True)
            take = cnt >= k_rem
            cand_ref[0:nt] = cand & (plane ^ jnp.where(take, jnp.int32(0), jnp.int32(-1)))
            bit = lax.shift_left(jnp.int32(1), 31 - p)
            return jnp.where(take, k_rem, k_rem - cnt), jnp.where(take, pat | bit, pat)

        _, pat = lax.fori_loop(0, 32, radix_body,
                               (jnp.full((1, qb), topk, jnp.int32), jnp.zeros((1, qb), jnp.int32)))
        sel_ref[3:4, :] = pltpu.bitcast(pat, F32)

    quarter = cand_ref.shape[0] // 4
    for v in range(1, 5):
        @pl.when(jnp.logical_and(n_tiles > (v - 1) * quarter, n_tiles <= v * quarter))
        def _(v=v):
            radix_select(v * quarter)

    t_u = pltpu.bitcast(sel_ref[3:4, :], jnp.int32)

    tile_pairs = (n_tiles + 1) // 2

    def publish(thr):
        def body(i, carry):
            n_gt, n_ge = carry
            for kt in (2 * i, 2 * i + 1):
                x = scores_ref[kt]
                ge = x >= thr
                mask_ref[kt] = jnp.where(ge, 0.0, MASK_VALUE).astype(BF16)
                n_gt = n_gt + jnp.sum(jnp.where(x > thr, 1.0, 0.0), axis=0, keepdims=True)
                n_ge = n_ge + jnp.sum(jnp.where(ge, 1.0, 0.0), axis=0, keepdims=True)
            return n_gt, n_ge
        zero = jnp.zeros((1, qb), F32)
        n_gt, n_ge = lax.fori_loop(0, tile_pairs, body, (zero, zero))
        sel_ref[0:1, :] = thr
        sel_ref[1:2, :] = n_gt
        sel_ref[2:3, :] = n_ge
        return n_gt, n_ge

    n_gt, n_ge = publish(_finite_threshold(t_u))
    few = (j * qb + lax.broadcasted_iota(jnp.int32, (1, qb), 1)) < topk - 1
    good = jnp.logical_or(few, jnp.logical_and(n_gt < topk, n_ge >= topk))
    all_good = jnp.min(jnp.where(good, 1, 0))

    @pl.when(all_good == 0)
    def _():
        def bit_body(i, pat_u):
            cand_u = pat_u | lax.shift_left(jnp.int32(1), 31 - i)
            cand_f = _pattern_to_float(cand_u)

            def cnt_body(kt, acc):
                return acc + jnp.sum(jnp.where(scores_ref[kt] >= cand_f, 1.0, 0.0), axis=0, keepdims=True)

            cnt = lax.fori_loop(0, n_tiles, cnt_body, jnp.zeros((1, qb), F32))
            return jnp.where(cnt >= topk, cand_u, pat_u)

        pat_u = lax.fori_loop(0, 32, bit_body, jnp.zeros((1, qb), jnp.int32))
        publish(_finite_threshold(pat_u))

    thr = sel_ref[0:1, :]
    need = topk - sel_ref[1:2, :]
    most_ge = jnp.max(sel_ref[2:3, :])

    @pl.when(most_ge > topk)
    def _():
        lower = (lax.broadcasted_iota(jnp.int32, (tk, tk), 0) >= lax.broadcasted_iota(jnp.int32, (tk, tk), 1))
        lower = jnp.where(lower, 1.0, 0.0).astype(BF16)

        def bias_body(kt, seen):
            x = scores_ref[kt]
            eq = x == thr
            eqf = jnp.where(eq, 1.0, 0.0).astype(BF16)
            rank = jnp.dot(lower, eqf, preferred_element_type=F32) + seen
            keep_tie = jnp.where(rank <= need, 0.0, MASK_VALUE)
            mask_ref[kt] = jnp.where(x > thr, 0.0, jnp.where(eq, keep_tie, MASK_VALUE)).astype(BF16)
            return rank[tk - 1:tk, :]

        lax.fori_loop(0, n_tiles, bias_body, jnp.zeros((1, qb), F32))

    m_ref[...] = jnp.full(m_ref.shape, NEG_INIT, F32)
    acc_ref[...] = jnp.zeros(acc_ref.shape, F32)
    empty_tile = 2 * tile_pairs
    mask_ref[empty_tile] = jnp.full((tk, qb), MASK_VALUE, BF16)
    max_tile = ckvT_ref.shape[1] - 1
    col_q = lax.broadcasted_iota(jnp.int32, (qb, n_heads * qb), 1) & (qb - 1)
    row_q = lax.broadcasted_iota(jnp.int32, (qb, n_heads * qb), 0)
    wq_ref[n_kv:n_kv + qb, :] = jnp.where(col_q == row_q, 1.0, 0.0).astype(BF16)

    pair = 2 * qb
    n_pairs = n_heads // 2
    spt = ATT_STEP_TILES
    n_steps = (n_tiles + spt - 1) // spt

    def key_rows(kt):
        k0 = pl.multiple_of(jnp.minimum(kt, max_tile) * tk, tk)
        return jnp.concatenate([ckv_ref[0, pl.ds(k0, tk), :], mask_ref[jnp.minimum(kt, empty_tile)]], axis=1)

    def produce(st, s_ref):
        lhs = jnp.concatenate([key_rows(spt * st + r) for r in range(spt)], axis=0)
        for hp in range(n_pairs):
            sl = slice(hp * pair, (hp + 1) * pair)
            s_ref[:, sl] = jnp.dot(lhs, wq_ref[:, sl], preferred_element_type=F32)

    def consume(st, s_ref):
        cv = jnp.concatenate([ckvT_ref[0, jnp.minimum(spt * st + r, max_tile)] for r in range(spt)], axis=1)
        for hp in range(n_pairs):
            sl = slice(hp * pair, (hp + 1) * pair)
            m_old = m_ref[:, sl]
            m_new = jnp.maximum(m_old, jnp.max(s_ref[:, sl], axis=0, keepdims=True))
            p = jnp.exp2(s_ref[:, sl] - m_new).astype(BF16)
            acc_ref[:, sl] = acc_ref[:, sl] * jnp.exp2(m_old - m_new) + jnp.dot(
                cv, p, preferred_element_type=F32)
            m_ref[:, sl] = m_new

    produce(0, sa_ref)

    def two_steps(base):
        produce(base + 1, sb_ref)
        consume(base, sa_ref)
        produce(base + 2, sa_ref)
        consume(base + 1, sb_ref)

    def eight_steps(i, carry):
        for r in range(4):
            two_steps(8 * i + 2 * r)
        return carry

    lax.fori_loop(0, n_steps // 8, eight_steps, 0)
    done = (n_steps // 8) * 8

    @pl.when(n_steps - done >= 4)
    def _():
        two_steps(done)
        two_steps(done + 2)

    done4 = (n_steps // 4) * 4

    @pl.when(n_steps - done4 >= 2)
    def _():
        two_steps(done4)

    @pl.when(n_steps % 2 == 1)
    def _():
        consume(n_steps - 1, sa_ref)

    inv_l = 1.0 / acc_ref[n_kv:n_kv + 1, :]
    parts = []
    for h in range(n_heads):
        o_h = (acc_ref[0:n_kv, h * qb:(h + 1) * qb] * inv_l[:, h * qb:(h + 1) * qb]).astype(BF16)
        parts.append(jnp.dot(wuvT_ref[h], o_h, preferred_element_type=F32))
    t_tok = jnp.concatenate(parts, axis=0).T.astype(BF16)
    o_ref[0] = jnp.dot(t_tok, wproj_ref[...], preferred_element_type=F32).astype(BF16)


def _dsa_branch(qT, qiT, wiT, ki, ckv, ckvT, w_uk, w_uv, w_proj, topk):
    bsz, n_q, seq = qT.shape
    n_kv, n_heads, d_head = w_uk.shape
    n_ki = ki.shape[2]
    n_iheads = qiT.shape[1] // n_ki
    d_out = w_proj.shape[1]
    qb, tk = Q_BLOCK, KEY_TILE
    wuk = jnp.transpose(w_uk, (1, 0, 2)).astype(BF16)
    wuvT = jnp.transpose(w_uv, (1, 2, 0)).astype(BF16)
    wproj = w_proj.astype(BF16)
    kern = functools.partial(_dsa_kernel, topk=topk, n_heads=n_heads, d_head=d_head,
                             n_iheads=n_iheads, d_ihead=n_ki, n_kv=n_kv)
    lane = lambda r: pl.BlockSpec((1, r, qb), lambda b, j: (b, 0, j))
    per_b = lambda a: pl.BlockSpec((1,) + a.shape[1:], lambda b, j: (b,) + (0,) * (a.ndim - 1))
    full = lambda a: pl.BlockSpec(a.shape, lambda b, j: (0,) * a.ndim)
    return pl.pallas_call(
        kern,
        grid=(bsz, seq // qb),
        in_specs=[lane(n_q), lane(qiT.shape[1]), lane(wiT.shape[1]), per_b(ki), per_b(ckv), per_b(ckvT),
                  full(wuk), full(wuvT), full(wproj)],
        out_specs=pl.BlockSpec((1, qb, d_out), lambda b, j: (b, j, 0)),
        out_shape=jax.ShapeDtypeStruct((bsz, seq, d_out), BF16),
        scratch_shapes=[pltpu.VMEM((seq // tk, tk, qb), F32),
                        pltpu.VMEM((seq // tk, 33, 8, qb), jnp.int32),
                        pltpu.VMEM((seq // tk, 8, qb), jnp.int32),
                        pltpu.VMEM((8, qb), F32),
                        pltpu.VMEM((seq // tk + 1, tk, qb), BF16),
                        pltpu.VMEM((n_kv + qb, n_heads * qb), BF16),
                        pltpu.VMEM((n_kv + ONES_ROWS, n_heads * qb), F32),
                        pltpu.VMEM((1, n_heads * qb), F32),
                        pltpu.VMEM((ATT_STEP_TILES * tk, n_heads * qb), F32),
                        pltpu.VMEM((ATT_STEP_TILES * tk, n_heads * qb), F32)],
        compiler_params=pltpu.CompilerParams(vmem_limit_bytes=VMEM_LIMIT),
        name="dsa",
    )(qT, qiT, wiT, ki, ckv, ckvT, wuk, wuvT, wproj)


def _ffn_kernel(x_ref, y_ref, a_ref, ga_ref, gb_ref, g1_ref, sh2_ref, sc2_ref, g2_ref,
                wglu_ref, bglu_ref, wout_ref, n2_ref, wg_ref, wu_ref, wd_ref, fg_ref, o_ref, *, bounds):
    x = x_ref[0]
    y_gelu = jnp.concatenate([y_ref[0, s] for s in range(y_ref.shape[1])], axis=1).astype(BF16)
    z = jnp.dot(y_gelu, wglu_ref[...], preferred_element_type=F32) + bglu_ref[...]
    d = z.shape[1] // 2
    y_ssm = z[:, :d] * jax.nn.sigmoid(z[:, d:])
    merged = (jax.nn.sigmoid(ga_ref[0].astype(F32)) * y_ssm
              + jax.nn.sigmoid(gb_ref[0].astype(F32)) * a_ref[0].astype(F32))
    h1 = x + g1_ref[0] * jnp.dot(merged.astype(BF16), wout_ref[...], preferred_element_type=F32)
    u2 = _rms(h1, -1) * n2_ref[...]
    u2 = (u2 * (1.0 + sc2_ref[0]) + sh2_ref[0]).astype(BF16)
    ffn = None
    for lo, hi in zip(bounds[:-1], bounds[1:]):
        sl = slice(lo, hi)
        gate = jnp.dot(u2, wg_ref[:, sl], preferred_element_type=F32)
        up = jnp.dot(u2, wu_ref[:, sl], preferred_element_type=F32)
        act = ((gate * jax.nn.sigmoid(gate)) * up).astype(BF16)
        part = jnp.dot(act, wd_ref[sl, :], preferred_element_type=F32)
        ffn = part if ffn is None else ffn + part
    h2 = h1 + g2_ref[0] * ffn
    o_ref[0] = _rms(h2, -1) * fg_ref[...]


def _merge_ffn(x, y_gelu, y_att, ga, gb, gate1, shift2, scale2, gate2,
               w_glu, b_glu, w_out, norm2_g, w_gate, w_up, w_down, final_g):
    bsz, seq, d = x.shape
    tm = TOKEN_TILE
    hidden = w_gate.shape[1]
    bounds = (0, -(-hidden // (2 * MXU_TILE)) * MXU_TILE, hidden) if hidden > 2 * MXU_TILE else (0, hidden)
    tok = lambda w: pl.BlockSpec((1, tm, w), lambda b, t: (b, t, 0))
    vec = lambda w: pl.BlockSpec((1, 1, w), lambda b, t: (b, 0, 0))
    const = lambda a: pl.BlockSpec(a.shape, lambda b, t: (0,) * a.ndim, pipeline_mode=pl.Buffered(1))
    wglu = w_glu.astype(BF16)
    bglu = b_glu.reshape(1, -1)
    wout = w_out.astype(BF16)
    n2 = norm2_g.reshape(1, d)
    wg = w_gate.astype(BF16)
    wu = w_up.astype(BF16)
    wd = w_down.astype(BF16)
    fg = final_g.reshape(1, d)
    return pl.pallas_call(
        functools.partial(_ffn_kernel, bounds=bounds),
        grid=(bsz, seq // tm),
        in_specs=[tok(d), pl.BlockSpec((1, y_gelu.shape[1], tm, LANES), lambda b, t: (b, 0, t, 0)),
                  tok(d), tok(d), tok(d), vec(d), vec(d), vec(d), vec(d),
                  const(wglu), const(bglu), const(wout), const(n2), const(wg), const(wu), const(wd), const(fg)],
        out_specs=tok(d),
        out_shape=jax.ShapeDtypeStruct((bsz, seq, d), F32),
        compiler_params=pltpu.CompilerParams(vmem_limit_bytes=VMEM_LIMIT),
        name="merge_ffn",
    )(x, y_gelu, y_att, ga, gb, gate1, shift2, scale2, gate2, wglu, bglu, wout, n2, wg, wu, wd, fg)


def kernel(x, c, w_mod, b_mod, norm1_g, w_in, ssm_a_re, ssm_a_im, ssm_log_dt, ssm_b_re, ssm_b_im, ssm_c_re, ssm_c_im, ssm_d, w_ssm_glu, b_ssm_glu, kv_norm_g, idx_k_norm_g, w_uk, w_uv, w_attn_proj, w_out, norm2_g, w_ffn_gate, w_ffn_up, w_ffn_down, final_g):
    bsz, seq, d = x.shape
    assert w_mod.shape[0] == 1, "single-layer block"
    assert seq % TOKEN_TILE == 0 and seq % (4 * KEY_TILE) == 0 and TOKEN_TILE % KEY_TILE == 0
    g, p = ssm_a_re.shape[1:]
    cg = ssm_b_re.shape[3]
    assert cg == SSM_GROUP and g % 2 == 0
    n_xs = g * cg
    n_kv, n_heads, d_head = w_uk.shape[1:]
    n_q = n_heads * d_head
    n_ki = idx_k_norm_g.shape[1]
    n_wi = w_in.shape[2] - (n_xs + n_q + n_kv + n_ki + 2 * d)
    n_wi = n_wi // (n_ki + 1)
    n_qi = n_wi * n_ki
    topk = min(TOPK_MAX, seq // 4)

    mod = _modulation(c, w_mod[0], b_mod[0])
    shift1, scale1, gate1, shift2, scale2, gate2 = [m.reshape(bsz, 1, d) for m in jnp.split(mod, 6, axis=-1)]

    offs = {}
    o = 0
    for name, size in (("xs", n_xs), ("q", n_q), ("ckv", n_kv), ("qi", n_qi), ("ki", n_ki), ("wi", n_wi),
                       ("ga", d), ("gb", d)):
        offs[name] = (o, o + size)
        o += size
    assert o == w_in.shape[2]
    wi_all = w_in[0]
    cols = lambda n: wi_all[:, offs[n][0]:offs[n][1]]
    wi_rows = -(-n_wi // 8) * 8
    w_nat = jnp.concatenate([cols("xs"), cols("ga"), cols("gb"), cols("ckv"), cols("ki")], axis=1).astype(BF16)
    w_tr = jnp.concatenate([cols("q"), cols("qi"), cols("ckv"),
                            jnp.pad(cols("wi"), ((0, 0), (0, wi_rows - n_wi)))], axis=1).T.astype(BF16)
    xs, ga, gb, ckv, ki, qT, qiT, ckvT, wiT = _in_projection(
        x, scale1, shift1, norm1_g[0], w_nat, w_tr, kv_norm_g[0], idx_k_norm_g[0],
        (n_xs, d, n_kv, n_ki, n_q, n_qi, wi_rows))

    prep = _ssm_prep(ssm_a_re[0], ssm_a_im[0], ssm_log_dt[0], ssm_b_re[0], ssm_b_im[0], ssm_c_re[0], ssm_c_im[0])
    y_gelu = _s5_branch(xs, _ssm_matrices(*prep, ssm_d[0]))

    y_att = _dsa_branch(qT, qiT, wiT, ki, ckv, ckvT, w_uk[0], w_uv[0], w_attn_proj[0], topk)

    return _merge_ffn(x, y_gelu, y_att, ga, gb, gate1, shift2, scale2, gate2,
                      w_ssm_glu[0], b_ssm_glu[0], w_out[0], norm2_g[0],
                      w_ffn_gate[0], w_ffn_up[0], w_ffn_down[0], final_g)
```

```python
import functools

import jax
import jax.numpy as jnp
from jax import lax
from jax.experimental import pallas as pl
from jax.experimental.pallas import tpu as pltpu

F32 = jnp.float32
BF16 = jnp.bfloat16
HIGHEST = lax.Precision.HIGHEST

EPS = 1e-6
SSM_GROUP = 16
SSM_CHUNK = 16
TOPK_MAX = 256
LANES = 128
MXU_TILE = 256
Q_BLOCK = 128
KEY_TILE = 256
ATT_STEP_TILES = 1
TOKEN_TILE = 512
ONES_ROWS = 8
INT_MIN = -(2 ** 31)
MASK_VALUE = -2.0 ** 100
NEG_INIT = -2.0 ** 99
LOG2E = 1.4426950408889634
VMEM_LIMIT = 56 * 1024 * 1024


def _rms(x, axis):
    return x * lax.rsqrt(jnp.mean(x * x, axis=axis, keepdims=True) + EPS)


def _mod_kernel(c_ref, w_ref, b_ref, o_ref):
    cv = c_ref[...]
    cond = cv * jax.nn.sigmoid(cv)
    o_ref[...] = jnp.dot(cond, w_ref[...], precision=HIGHEST, preferred_element_type=F32) + b_ref[...]


def _modulation(c, w_mod, b_mod):
    bsz, d = c.shape
    n = w_mod.shape[1]
    rows = -(-bsz // 8) * 8
    cp = jnp.pad(c, ((0, rows - bsz), (0, 0)))
    tn = d
    out = pl.pallas_call(
        _mod_kernel,
        grid=(n // tn,),
        in_specs=[pl.BlockSpec((rows, d), lambda j: (0, 0)),
                  pl.BlockSpec((d, tn), lambda j: (0, j)),
                  pl.BlockSpec((1, tn), lambda j: (0, j))],
        out_specs=pl.BlockSpec((rows, tn), lambda j: (0, j)),
        out_shape=jax.ShapeDtypeStruct((rows, n), F32),
        name="modulation",
    )(cp, w_mod, b_mod.reshape(1, n))
    return out[:bsz]


def _inproj_kernel(x_ref, sc_ref, sh_ref, g1_ref, wn_ref, wt_ref, gkv_ref, gkvc_ref, gki_ref,
                   xs_ref, ga_ref, gb_ref, ckv_ref, ki_ref, qT_ref, qiT_ref, ckvT_ref, wiT_ref,
                   *, n_xs, n_gate, n_kv, n_ki, n_q, n_qi, n_wi):
    x = x_ref[0]
    u = _rms(x, -1) * g1_ref[...]
    u = u * (1.0 + sc_ref[0]) + sh_ref[0]
    ub = u.astype(BF16)
    tm = ub.shape[0]

    zn = jnp.dot(ub, wn_ref[...], preferred_element_type=F32)
    o = 0
    for s in range(n_xs // LANES):
        xs_ref[0, s] = zn[:, s * LANES:(s + 1) * LANES]
    o += n_xs
    ga_ref[0] = zn[:, o:o + n_gate].astype(BF16)
    o += n_gate
    gb_ref[0] = zn[:, o:o + n_gate].astype(BF16)
    o += n_gate
    ckv_ref[0] = (_rms(zn[:, o:o + n_kv], -1) * gkv_ref[...]).astype(BF16)
    o += n_kv
    ki_ref[0] = (_rms(zn[:, o:o + n_ki], -1) * gki_ref[...]).astype(BF16)

    zt = lax.dot_general(wt_ref[...], ub, (((1,), (1,)), ((), ())), preferred_element_type=F32)
    o = 0
    qT_ref[0] = zt[o:o + n_q].astype(BF16)
    o += n_q
    qiT_ref[0] = zt[o:o + n_qi].astype(BF16)
    o += n_qi
    ct = (_rms(zt[o:o + n_kv], 0) * gkvc_ref[...]).astype(BF16)
    o += n_kv
    wiT_ref[0] = zt[o:o + n_wi]
    ones_blk = jnp.where(lax.broadcasted_iota(jnp.int32, (ONES_ROWS, KEY_TILE), 0) == 0, 1.0, 0.0).astype(BF16)
    for i in range(tm // KEY_TILE):
        ckvT_ref[0, i, 0:n_kv, :] = ct[:, i * KEY_TILE:(i + 1) * KEY_TILE]
        ckvT_ref[0, i, n_kv:n_kv + ONES_ROWS, :] = ones_blk


def _in_projection(x, scale1, shift1, norm1_g, w_nat, w_tr, kv_g, ki_g, sizes):
    bsz, seq, d = x.shape
    n_xs, n_gate, n_kv, n_ki, n_q, n_qi, n_wi = sizes
    tm = TOKEN_TILE
    nt = seq // tm
    kern = functools.partial(_inproj_kernel, n_xs=n_xs, n_gate=n_gate, n_kv=n_kv, n_ki=n_ki,
                             n_q=n_q, n_qi=n_qi, n_wi=n_wi)
    tok = lambda w: pl.BlockSpec((1, tm, w), lambda b, t: (b, t, 0))
    lane = lambda r: pl.BlockSpec((1, r, tm), lambda b, t: (b, 0, t))
    vec = lambda w: pl.BlockSpec((1, 1, w), lambda b, t: (b, 0, 0))
    full = lambda a: pl.BlockSpec(a.shape, lambda b, t: (0,) * a.ndim)
    g1 = norm1_g.reshape(1, d)
    gkv = kv_g.reshape(1, n_kv)
    gkvc = kv_g.reshape(n_kv, 1)
    gki = ki_g.reshape(1, n_ki)
    out_shape = (
        jax.ShapeDtypeStruct((bsz, n_xs // LANES, seq, LANES), F32),
        jax.ShapeDtypeStruct((bsz, seq, n_gate), BF16),
        jax.ShapeDtypeStruct((bsz, seq, n_gate), BF16),
        jax.ShapeDtypeStruct((bsz, seq, n_kv), BF16),
        jax.ShapeDtypeStruct((bsz, seq, n_ki), BF16),
        jax.ShapeDtypeStruct((bsz, n_q, seq), BF16),
        jax.ShapeDtypeStruct((bsz, n_qi, seq), BF16),
        jax.ShapeDtypeStruct((bsz, seq // KEY_TILE, n_kv + ONES_ROWS, KEY_TILE), BF16),
        jax.ShapeDtypeStruct((bsz, n_wi, seq), F32),
    )
    out_specs = (
        pl.BlockSpec((1, n_xs // LANES, tm, LANES), lambda b, t: (b, 0, t, 0)),
        tok(n_gate), tok(n_gate), tok(n_kv), tok(n_ki),
        lane(n_q), lane(n_qi),
        pl.BlockSpec((1, tm // KEY_TILE, n_kv + ONES_ROWS, KEY_TILE), lambda b, t: (b, t, 0, 0)),
        lane(n_wi),
    )
    return pl.pallas_call(
        kern,
        grid=(bsz, nt),
        in_specs=[tok(d), vec(d), vec(d), full(g1), full(w_nat), full(w_tr), full(gkv), full(gkvc), full(gki)],
        out_specs=out_specs,
        out_shape=out_shape,
        compiler_params=pltpu.CompilerParams(vmem_limit_bytes=VMEM_LIMIT),
        name="in_projection",
    )(x, scale1, shift1, g1, w_nat, w_tr, gkv, gkvc, gki)


def _ssm_prep_kernel(are_ref, aim_ref, ldt_ref, bre_ref, bim_ref, cre_ref, cim_ref, ctre_ref, ctim_ref,
                     kall_ref, dre_ref, dim_ref, vre_ref, vimn_ref, lre_ref, lim_ref):
    for gi in range(are_ref.shape[0]):
        _ssm_prep_group(gi, are_ref, aim_ref, ldt_ref, bre_ref, bim_ref, cre_ref, cim_ref, ctre_ref, ctim_ref,
                        kall_ref, dre_ref, dim_ref, vre_ref, vimn_ref, lre_ref, lim_ref)


def _ssm_prep_group(gi, are_ref, aim_ref, ldt_ref, bre_ref, bim_ref, cre_ref, cim_ref, ctre_ref, ctim_ref,
                    kall_ref, dre_ref, dim_ref, vre_ref, vimn_ref, lre_ref, lim_ref):
    ar = are_ref[gi]
    ai = aim_ref[gi]
    dt = jnp.exp(ldt_ref[gi])
    ard = ar * dt
    aid = ai * dt

    def lam_pow(kf):
        mag = jnp.exp(ard * kf)
        ang = aid * kf
        return mag * jnp.cos(ang), mag * jnp.sin(ang)

    lb_re, lb_im = lam_pow(1.0)
    den = ar * ar + ai * ai
    coef_re = ((lb_re - 1.0) * ar + lb_im * ai) / den
    coef_im = (lb_im * ar - (lb_re - 1.0) * ai) / den
    bre = bre_ref[gi]
    bim = bim_ref[gi]
    bb_re = coef_re * bre - coef_im * bim
    bb_im = coef_re * bim + coef_im * bre

    width = bre.shape[1]
    lag = lax.shift_right_logical(lax.broadcasted_iota(jnp.int32, (1, width), 1), 4).astype(F32)
    lk_re, lk_im = lam_pow(lag)
    d_re = lk_re * bb_re - lk_im * bb_im
    d_im = lk_re * bb_im + lk_im * bb_re
    dre_ref[gi] = d_re
    dim_ref[gi] = d_im
    kall_ref[gi] = (jnp.dot(cre_ref[gi], d_re, precision=HIGHEST, preferred_element_type=F32)
                    - jnp.dot(cim_ref[gi], d_im, precision=HIGHEST, preferred_element_type=F32))
    l1_re, l1_im = lam_pow(lag + 1.0)
    ctre = ctre_ref[gi]
    ctim = ctim_ref[gi]
    vre_ref[gi] = l1_re * ctre - l1_im * ctim
    vimn_ref[gi] = -(l1_re * ctim + l1_im * ctre)
    lt_re, lt_im = lam_pow(float(SSM_CHUNK))
    lre_ref[gi] = lt_re
    lim_ref[gi] = lt_im


def _ssm_prep(a_re, a_im, log_dt, b_re, b_im, c_re, c_im):
    g, p = a_re.shape
    cg = b_re.shape[2]
    width = SSM_CHUNK * cg
    col = lambda a: a.reshape(g, p, 1)
    b_t = lambda b: jnp.tile(b, (1, 1, SSM_CHUNK))
    c_t = lambda cc: jnp.tile(jnp.swapaxes(cc, 1, 2), (1, 1, SSM_CHUNK))
    gpb = LANES // cg
    blk = lambda s: pl.BlockSpec((gpb,) + s, lambda i: (i, 0, 0))
    outs = pl.pallas_call(
        _ssm_prep_kernel,
        grid=(g // gpb,),
        in_specs=[blk((p, 1)), blk((p, 1)), blk((1, 1)), blk((p, width)), blk((p, width)),
                  blk((cg, p)), blk((cg, p)), blk((p, width)), blk((p, width))],
        out_specs=(blk((cg, width)), blk((p, width)), blk((p, width)), blk((p, width)), blk((p, width)),
                   blk((p, 1)), blk((p, 1))),
        out_shape=(jax.ShapeDtypeStruct((g, cg, width), F32),) + (jax.ShapeDtypeStruct((g, p, width), F32),) * 4
        + (jax.ShapeDtypeStruct((g, p, 1), F32),) * 2,
        name="ssm_prep",
    )(col(a_re), col(a_im), log_dt.reshape(g, 1, 1), b_t(b_re), b_t(b_im), c_re, c_im, c_t(c_re), c_t(c_im))
    return outs


def _ssm_matrices(kall, d_re, d_im, v_re, v_imn, l_re, l_im, d_skip):
    g, cg, width = kall.shape
    p = d_re.shape[1]
    t = SSM_CHUNK
    gs = LANES // cg
    ns = g // gs
    kflip = kall.reshape(g, cg, t, cg)[:, :, ::-1, :].reshape(g, cg, width)
    kz = jnp.pad(kflip, ((0, 0), (0, 0), (0, width)))
    m = jnp.stack([kz[:, :, cg * (t - 1 - ti):cg * (t - 1 - ti) + width] for ti in range(t)], axis=1)
    m = m.reshape(ns, gs, width, width).astype(BF16)
    w_of = lambda d: d.reshape(g, p, t, cg)[:, :, ::-1, :].reshape(g, p, width)
    w = jnp.concatenate([w_of(d_re), w_of(d_im)], axis=1).reshape(ns, gs, 2 * p, width).astype(BF16)
    v = jnp.concatenate([jnp.swapaxes(v_re, 1, 2), jnp.swapaxes(v_imn, 1, 2)], axis=2)
    v = v.reshape(ns, gs, width, 2 * p).astype(BF16)
    lam = jnp.concatenate([l_re.reshape(ns, 1, gs * p), l_im.reshape(ns, 1, gs * p)], axis=1)
    dsk = jnp.broadcast_to(d_skip.reshape(g, 1, cg), (g, t, cg)).reshape(ns, gs, width, 1)
    return m, w, v, lam, dsk


def _gelu_tanh(y):
    return 0.5 * y * (1.0 + jnp.tanh(0.7978845608028654 * (y + 0.044715 * (y * y * y))))


def _ssm_kernel(xs_ref, m_ref, w_ref, v_ref, lam_ref, d_ref, y_ref,
                zt_ref, ure_ref, uim_ref, sre_ref, sim_ref, yt_ref):
    t = SSM_CHUNK
    cg = SSM_GROUP
    gs = zt_ref.shape[0]
    nch = zt_ref.shape[2]
    p = w_ref.shape[2] // 2

    for ti in range(t):
        xt = xs_ref[0, 0, pl.ds(ti, nch, stride=t), :].T
        for g in range(gs):
            zt_ref[g, ti * cg:(ti + 1) * cg, :] = xt[g * cg:(g + 1) * cg, :].astype(BF16)

    ut = [jnp.dot(w_ref[0, g], zt_ref[g], preferred_element_type=F32) for g in range(gs)]
    ure_ref[...] = jnp.concatenate([u[:p] for u in ut], axis=0).T
    uim_ref[...] = jnp.concatenate([u[p:] for u in ut], axis=0).T

    lr = lam_ref[0, 0:1, :]
    li = lam_ref[0, 1:2, :]

    def scan_body(i, carry):
        s_re, s_im = carry
        sre_ref[pl.ds(i, 1), :] = s_re
        sim_ref[pl.ds(i, 1), :] = s_im
        u_re = ure_ref[pl.ds(i, 1), :]
        u_im = uim_ref[pl.ds(i, 1), :]
        return (lr * s_re - li * s_im + u_re, lr * s_im + li * s_re + u_im)

    zero = jnp.zeros_like(lr)
    lax.fori_loop(0, nch, scan_body, (zero, zero))

    sre_t = sre_ref[...].T
    sim_t = sim_ref[...].T
    for g in range(gs):
        z = zt_ref[g]
        s = jnp.concatenate([sre_t[g * p:(g + 1) * p], sim_t[g * p:(g + 1) * p]], axis=0).astype(BF16)
        y = (jnp.dot(m_ref[0, g], z, preferred_element_type=F32)
             + jnp.dot(v_ref[0, g], s, preferred_element_type=F32)
             + d_ref[0, g] * z.astype(F32))
        yt_ref[g] = _gelu_tanh(y)

    for ti in range(t):
        blk = jnp.concatenate([yt_ref[g, ti * cg:(ti + 1) * cg, :] for g in range(gs)], axis=0)
        y_ref[0, 0, pl.ds(ti, nch, stride=t), :] = blk.T


def _s5_branch(xs, mats):
    m, w, v, lam, dsk = mats
    bsz, ns, seq, lanes = xs.shape
    gs = m.shape[1]
    width = m.shape[2]
    p2 = w.shape[2]
    nch = seq // SSM_CHUNK
    slab = pl.BlockSpec((1, 1, seq, lanes), lambda b, s: (b, s, 0, 0))
    per_slab = lambda a: pl.BlockSpec((1,) + a.shape[1:], lambda b, s: (s,) + (0,) * (a.ndim - 1))
    return pl.pallas_call(
        _ssm_kernel,
        grid=(bsz, ns),
        in_specs=[slab, per_slab(m), per_slab(w), per_slab(v), per_slab(lam), per_slab(dsk)],
        out_specs=slab,
        out_shape=jax.ShapeDtypeStruct(xs.shape, F32),
        scratch_shapes=[pltpu.VMEM((gs, width, nch), BF16),
                        pltpu.VMEM((nch, gs * p2 // 2), F32), pltpu.VMEM((nch, gs * p2 // 2), F32),
                        pltpu.VMEM((nch, gs * p2 // 2), F32), pltpu.VMEM((nch, gs * p2 // 2), F32),
                        pltpu.VMEM((gs, width, nch), F32)],
        compiler_params=pltpu.CompilerParams(vmem_limit_bytes=VMEM_LIMIT),
        name="ssm",
    )(xs, m, w, v, lam, dsk)


def _i32(v):
    return jnp.int32(v - (1 << 32) if v >= (1 << 31) else v)


def _order_pattern(x):
    bits = pltpu.bitcast(x, jnp.int32)
    return bits ^ (lax.shift_right_arithmetic(bits, 31) | jnp.int32(INT_MIN))


def _pattern_to_float(u):
    bits = jnp.where(u < 0, u & jnp.int32(0x7FFFFFFF), ~u)
    return pltpu.bitcast(bits, F32)


def _finite_threshold(u):
    return _pattern_to_float(jnp.where(u < 0, u, jnp.maximum(u, jnp.int32(0x00800000))))


def _bit_transpose32(words):
    a = list(words)
    j, m = 16, 0x0000FFFF
    while j:
        k = 0
        while k < 32:
            t = (a[k] ^ lax.shift_right_logical(a[k + j], j)) & _i32(m)
            a[k] = a[k] ^ t
            a[k + j] = a[k + j] ^ lax.shift_left(t, j)
            k = (k + j + 1) & ~j
        j >>= 1
        m = (m ^ (m << j)) & 0xFFFFFFFF
    return a


def _dsa_kernel(qT_ref, qiT_ref, wiT_ref, ki_ref, ckv_ref, ckvT_ref, wuk_ref, wuvT_ref, wproj_ref,
                o_ref, scores_ref, planes_ref, cand_ref, sel_ref, mask_ref, wq_ref, acc_ref, m_ref,
                sa_ref, sb_ref,
                *, topk, n_heads, d_head, n_iheads, d_ihead, n_kv):
    qb = Q_BLOCK
    tk = KEY_TILE
    words = tk // 8
    j = pl.program_id(1)
    n_tiles = (j * qb + qb + tk - 1) // tk

    for h in range(n_heads):
        qh = qT_ref[0, h * d_head:(h + 1) * d_head, :]
        ql = jnp.dot(wuk_ref[h], qh, preferred_element_type=F32) * (d_head ** -0.5 * LOG2E)
        wq_ref[0:n_kv, h * qb:(h + 1) * qb] = ql.astype(BF16)
    qi_cat = jnp.concatenate([qiT_ref[0, h * d_ihead:(h + 1) * d_ihead, :] for h in range(n_iheads)], axis=1)
    w_heads = wiT_ref[0] * (n_iheads ** -0.5 * d_ihead ** -0.5)

    q_pos = j * qb + lax.broadcasted_iota(jnp.int32, (tk, qb), 1)
    row_iota = lax.broadcasted_iota(jnp.int32, (tk, qb), 0)

    def score_tile(kt, diagonal):
        k0 = pl.multiple_of(kt * tk, tk)
        rel = jnp.dot(ki_ref[0, pl.ds(k0, tk), :], qi_cat, preferred_element_type=F32)
        sc = None
        for h in range(n_iheads):
            term = jnp.maximum(rel[:, h * qb:(h + 1) * qb], 0.0) * w_heads[h:h + 1, :]
            sc = term if sc is None else sc + term
        if diagonal:
            sc = jnp.where((k0 + row_iota) <= q_pos, sc, -jnp.inf)
        scores_ref[kt] = sc
        bits = pltpu.bitcast(sc, jnp.int32)
        planes = _bit_transpose32([bits[8 * v:8 * v + 8] for v in range(words)])
        sign = planes[0]
        planes_ref[kt, 0] = ~sign
        for p in range(1, 32):
            planes_ref[kt, p] = planes[p] ^ sign

    @pl.when(jnp.logical_and(pl.program_id(0) == 0, j == 0))
    def _():
        planes_ref[...] = jnp.zeros(planes_ref.shape, jnp.int32)

    def score_pair(i):
        score_tile(2 * i, False)
        score_tile(2 * i + 1, False)

    def score_body(i, carry):
        score_pair(2 * i)
        score_pair(2 * i + 1)
        return carry

    last_pair = (n_tiles + 1) // 2 - 1
    lax.fori_loop(0, last_pair // 2, score_body, 0)

    @pl.when(last_pair % 2 == 1)
    def _():
        score_pair(last_pair - 1)

    score_tile(2 * last_pair, True)
    score_tile(2 * last_pair + 1, True)

    def radix_select(nt):
        active = lax.broadcasted_iota(jnp.int32, (nt, 8, qb), 0) < n_tiles
        cand_ref[0:nt] = jnp.where(active, jnp.int32(-1), jnp.int32(0))

        def total(words):
            return jnp.sum(jnp.sum(lax.population_count(words), axis=0), axis=0, keepdims=True)

        def radix_body(i, carry):
            k_rem, pat = carry
            hi_plane = planes_ref[0:nt, 2 * i]
            lo_plane = planes_ref[0:nt, 2 * i + 1]
            cand = cand_ref[0:nt]
            with_hi = cand & hi_plane
            without_hi = cand ^ with_hi
            n_hi = total(with_hi)
            n_hi_lo = total(with_hi & lo_plane)
            n_lo = total(without_hi & lo_plane)
            take_hi = n_hi >= k_rem
            k_mid = jnp.where(take_hi, k_rem, k_rem - n_hi)
            n_next = jnp.where(take_hi, n_hi_lo, n_lo)
            take_lo = n_next >= k_mid
            keep = jnp.where(take_hi, with_hi, without_hi)
            cand_ref[0:nt] = keep & (lo_plane ^ jnp.where(take_lo, jnp.int32(0), jnp.int32(-1)))
            hi_bit = lax.shift_left(jnp.int32(1), 31 - 2 * i)
            lo_bit = lax.shift_left(jnp.int32(1), 30 - 2 * i)
            pat = jnp.where(take_hi, pat | hi_bit, pat)
            pat = jnp.where(take_lo, pat | lo_bit, pat)
            return jnp.where(take_lo, k_mid, k_mid - n_next), pat

        _, pat = lax.fori_loop(0, 16, radix_body,
                               (jnp.full((1, qb), topk, jnp.int32), jnp.zeros((1, qb), jnp.int32)))
        sel_ref[3:4, :] = pltpu.bitcast(pat, F32)

    quarter = cand_ref.shape[0] // 4
    for v in range(1, 5):
        @pl.when(jnp.logical_and(n_tiles > (v - 1) * quarter, n_tiles <= v * quarter))
        def _(v=v):
            radix_select(v * quarter)

    t_u = pltpu.bitcast(sel_ref[3:4, :], jnp.int32)

    tile_pairs = (n_tiles + 1) // 2

    def publish(thr):
        def body(i, carry):
            n_gt, n_ge = carry
            for kt in (2 * i, 2 * i + 1):
                x = scores_ref[kt]
                ge = x >= thr
                mask_ref[kt] = jnp.where(ge, 0.0, MASK_VALUE).astype(BF16)
                n_gt = n_gt + jnp.sum(jnp.where(x > thr, 1.0, 0.0), axis=0, keepdims=True)
                n_ge = n_ge + jnp.sum(jnp.where(ge, 1.0, 0.0), axis=0, keepdims=True)
            return n_gt, n_ge
        zero = jnp.zeros((1, qb), F32)
        n_gt, n_ge = lax.fori_loop(0, tile_pairs, body, (zero, zero))
        sel_ref[0:1, :] = thr
        sel_ref[1:2, :] = n_gt
        sel_ref[2:3, :] = n_ge
        return n_gt, n_ge

    n_gt, n_ge = publish(_finite_threshold(t_u))
    few = (j * qb + lax.broadcasted_iota(jnp.int32, (1, qb), 1)) < topk - 1
    good = jnp.logical_or(few, jnp.logical_and(n_gt < topk, n_ge >= topk))
    all_good = jnp.min(jnp.where(good, 1, 0))

    @pl.when(all_good == 0)
    def _():
        def bit_body(i, pat_u):
            cand_u = pat_u | lax.shift_left(jnp.int32(1), 31 - i)
            cand_f = _pattern_to_float(cand_u)

            def cnt_body(kt, acc):
                return acc + jnp.sum(jnp.where(scores_ref[kt] >= cand_f, 1.0, 0.0), axis=0, keepdims=True)

            cnt = lax.fori_loop(0, n_tiles, cnt_body, jnp.zeros((1, qb), F32))
            return jnp.where(cnt >= topk, cand_u, pat_u)

        pat_u = lax.fori_loop(0, 32, bit_body, jnp.zeros((1, qb), jnp.int32))
        publish(_finite_threshold(pat_u))

    thr = sel_ref[0:1, :]
    need = topk - sel_ref[1:2, :]
    most_ge = jnp.max(sel_ref[2:3, :])

    @pl.when(most_ge > topk)
    def _():
        lower = (lax.broadcasted_iota(jnp.int32, (tk, tk), 0) >= lax.broadcasted_iota(jnp.int32, (tk, tk), 1))
        lower = jnp.where(lower, 1.0, 0.0).astype(BF16)

        def bias_body(kt, seen):
            x = scores_ref[kt]
            eq = x == thr
            eqf = jnp.where(eq, 1.0, 0.0).astype(BF16)
            rank = jnp.dot(lower, eqf, preferred_element_type=F32) + seen
            keep_tie = jnp.where(rank <= need, 0.0, MASK_VALUE)
            mask_ref[kt] = jnp.where(x > thr, 0.0, jnp.where(eq, keep_tie, MASK_VALUE)).astype(BF16)
            return rank[tk - 1:tk, :]

        lax.fori_loop(0, n_tiles, bias_body, jnp.zeros((1, qb), F32))

    m_ref[...] = jnp.full(m_ref.shape, NEG_INIT, F32)
    acc_ref[...] = jnp.zeros(acc_ref.shape, F32)
    empty_tile = 2 * tile_pairs
    mask_ref[empty_tile] = jnp.full((tk, qb), MASK_VALUE, BF16)
    max_tile = ckvT_ref.shape[1] - 1
    col_q = lax.broadcasted_iota(jnp.int32, (qb, n_heads * qb), 1) & (qb - 1)
    row_q = lax.broadcasted_iota(jnp.int32, (qb, n_heads * qb), 0)
    wq_ref[n_kv:n_kv + qb, :] = jnp.where(col_q == row_q, 1.0, 0.0).astype(BF16)

    pair = 2 * qb
    n_pairs = n_heads // 2
    spt = ATT_STEP_TILES
    n_steps = (n_tiles + spt - 1) // spt

    def key_rows(kt):
        k0 = pl.multiple_of(jnp.minimum(kt, max_tile) * tk, tk)
        return jnp.concatenate([ckv_ref[0, pl.ds(k0, tk), :], mask_ref[jnp.minimum(kt, empty_tile)]], axis=1)

    def produce(st, s_ref):
        lhs = jnp.concatenate([key_rows(spt * st + r) for r in range(spt)], axis=0)
        for hp in range(n_pairs):
            sl = slice(hp * pair, (hp + 1) * pair)
            s_ref[:, sl] = jnp.dot(lhs, wq_ref[:, sl], preferred_element_type=F32)

    def consume(st, s_ref):
        cv = jnp.concatenate([ckvT_ref[0, jnp.minimum(spt * st + r, max_tile)] for r in range(spt)], axis=1)
        for hp in range(n_pairs):
            sl = slice(hp * pair, (hp + 1) * pair)
            m_old = m_ref[:, sl]
            m_new = jnp.maximum(m_old, jnp.max(s_ref[:, sl], axis=0, keepdims=True))
            p = jnp.exp2(s_ref[:, sl] - m_new).astype(BF16)
            acc_ref[:, sl] = acc_ref[:, sl] * jnp.exp2(m_old - m_new) + jnp.dot(
                cv, p, preferred_element_type=F32)
            m_ref[:, sl] = m_new

    produce(0, sa_ref)

    def two_steps(base):
        produce(base + 1, sb_ref)
        consume(base, sa_ref)
        produce(base + 2, sa_ref)
        consume(base + 1, sb_ref)

    def eight_steps(i, carry):
        for r in range(4):
            two_steps(8 * i + 2 * r)
        return carry

    lax.fori_loop(0, n_steps // 8, eight_steps, 0)
    done = (n_steps // 8) * 8

    @pl.when(n_steps - done >= 4)
    def _():
        two_steps(done)
        two_steps(done + 2)

    done4 = (n_steps // 4) * 4

    @pl.when(n_steps - done4 >= 2)
    def _():
        two_steps(done4)

    @pl.when(n_steps % 2 == 1)
    def _():
        consume(n_steps - 1, sa_ref)

    inv_l = 1.0 / acc_ref[n_kv:n_kv + 1, :]
    parts = []
    for h in range(n_heads):
        o_h = (acc_ref[0:n_kv, h * qb:(h + 1) * qb] * inv_l[:, h * qb:(h + 1) * qb]).astype(BF16)
        parts.append(jnp.dot(wuvT_ref[h], o_h, preferred_element_type=F32))
    t_tok = jnp.concatenate(parts, axis=0).T.astype(BF16)
    o_ref[0] = jnp.dot(t_tok, wproj_ref[...], preferred_element_type=F32).astype(BF16)


def _dsa_branch(qT, qiT, wiT, ki, ckv, ckvT, w_uk, w_uv, w_proj, topk):
    bsz, n_q, seq = qT.shape
    n_kv, n_heads, d_head = w_uk.shape
    n_ki = ki.shape[2]
    n_iheads = qiT.shape[1] // n_ki
    d_out = w_proj.shape[1]
    qb, tk = Q_BLOCK, KEY_TILE
    wuk = jnp.transpose(w_uk, (1, 0, 2)).astype(BF16)
    wuvT = jnp.transpose(w_uv, (1, 2, 0)).astype(BF16)
    wproj = w_proj.astype(BF16)
    kern = functools.partial(_dsa_kernel, topk=topk, n_heads=n_heads, d_head=d_head,
                             n_iheads=n_iheads, d_ihead=n_ki, n_kv=n_kv)
    lane = lambda r: pl.BlockSpec((1, r, qb), lambda b, j: (b, 0, j))
    per_b = lambda a: pl.BlockSpec((1,) + a.shape[1:], lambda b, j: (b,) + (0,) * (a.ndim - 1))
    full = lambda a: pl.BlockSpec(a.shape, lambda b, j: (0,) * a.ndim)
    return pl.pallas_call(
        kern,
        grid=(bsz, seq // qb),
        in_specs=[lane(n_q), lane(qiT.shape[1]), lane(wiT.shape[1]), per_b(ki), per_b(ckv), per_b(ckvT),
                  full(wuk), full(wuvT), full(wproj)],
        out_specs=pl.BlockSpec((1, qb, d_out), lambda b, j: (b, j, 0)),
        out_shape=jax.ShapeDtypeStruct((bsz, seq, d_out), BF16),
        scratch_shapes=[pltpu.VMEM((seq // tk, tk, qb), F32),
                        pltpu.VMEM((seq // tk, 33, 8, qb), jnp.int32),
                        pltpu.VMEM((seq // tk, 8, qb), jnp.int32),
                        pltpu.VMEM((8, qb), F32),
                        pltpu.VMEM((seq // tk + 1, tk, qb), BF16),
                        pltpu.VMEM((n_kv + qb, n_heads * qb), BF16),
                        pltpu.VMEM((n_kv + ONES_ROWS, n_heads * qb), F32),
                        pltpu.VMEM((1, n_heads * qb), F32),
                        pltpu.VMEM((ATT_STEP_TILES * tk, n_heads * qb), F32),
                        pltpu.VMEM((ATT_STEP_TILES * tk, n_heads * qb), F32)],
        compiler_params=pltpu.CompilerParams(vmem_limit_bytes=VMEM_LIMIT),
        name="dsa",
    )(qT, qiT, wiT, ki, ckv, ckvT, wuk, wuvT, wproj)


def _ffn_kernel(x_ref, y_ref, a_ref, ga_ref, gb_ref, g1_ref, sh2_ref, sc2_ref, g2_ref,
                wglu_ref, bglu_ref, wout_ref, n2_ref, wg_ref, wu_ref, wd_ref, fg_ref, o_ref, *, bounds):
    x = x_ref[0]
    y_gelu = jnp.concatenate([y_ref[0, s] for s in range(y_ref.shape[1])], axis=1).astype(BF16)
    z = jnp.dot(y_gelu, wglu_ref[...], preferred_element_type=F32) + bglu_ref[...]
    d = z.shape[1] // 2
    y_ssm = z[:, :d] * jax.nn.sigmoid(z[:, d:])
    merged = (jax.nn.sigmoid(ga_ref[0].astype(F32)) * y_ssm
              + jax.nn.sigmoid(gb_ref[0].astype(F32)) * a_ref[0].astype(F32))
    h1 = x + g1_ref[0] * jnp.dot(merged.astype(BF16), wout_ref[...], preferred_element_type=F32)
    u2 = _rms(h1, -1) * n2_ref[...]
    u2 = (u2 * (1.0 + sc2_ref[0]) + sh2_ref[0]).astype(BF16)
    ffn = None
    for lo, hi in zip(bounds[:-1], bounds[1:]):
        sl = slice(lo, hi)
        gate = jnp.dot(u2, wg_ref[:, sl], preferred_element_type=F32)
        up = jnp.dot(u2, wu_ref[:, sl], preferred_element_type=F32)
        act = ((gate * jax.nn.sigmoid(gate)) * up).astype(BF16)
        part = jnp.dot(act, wd_ref[sl, :], preferred_element_type=F32)
        ffn = part if ffn is None else ffn + part
    h2 = h1 + g2_ref[0] * ffn
    o_ref[0] = _rms(h2, -1) * fg_ref[...]


def _merge_ffn(x, y_gelu, y_att, ga, gb, gate1, shift2, scale2, gate2,
               w_glu, b_glu, w_out, norm2_g, w_gate, w_up, w_down, final_g):
    bsz, seq, d = x.shape
    tm = TOKEN_TILE
    hidden = w_gate.shape[1]
    bounds = (0, -(-hidden // (2 * MXU_TILE)) * MXU_TILE, hidden) if hidden > 2 * MXU_TILE else (0, hidden)
    tok = lambda w: pl.BlockSpec((1, tm, w), lambda b, t: (b, t, 0))
    vec = lambda w: pl.BlockSpec((1, 1, w), lambda b, t: (b, 0, 0))
    const = lambda a: pl.BlockSpec(a.shape, lambda b, t: (0,) * a.ndim, pipeline_mode=pl.Buffered(1))
    wglu = w_glu.astype(BF16)
    bglu = b_glu.reshape(1, -1)
    wout = w_out.astype(BF16)
    n2 = norm2_g.reshape(1, d)
    wg = w_gate.astype(BF16)
    wu = w_up.astype(BF16)
    wd = w_down.astype(BF16)
    fg = final_g.reshape(1, d)
    return pl.pallas_call(
        functools.partial(_ffn_kernel, bounds=bounds),
        grid=(bsz, seq // tm),
        in_specs=[tok(d), pl.BlockSpec((1, y_gelu.shape[1], tm, LANES), lambda b, t: (b, 0, t, 0)),
                  tok(d), tok(d), tok(d), vec(d), vec(d), vec(d), vec(d),
                  const(wglu), const(bglu), const(wout), const(n2), const(wg), const(wu), const(wd), const(fg)],
        out_specs=tok(d),
        out_shape=jax.ShapeDtypeStruct((bsz, seq, d), F32),
        compiler_params=pltpu.CompilerParams(vmem_limit_bytes=VMEM_LIMIT),
        name="merge_ffn",
    )(x, y_gelu, y_att, ga, gb, gate1, shift2, scale2, gate2, wglu, bglu, wout, n2, wg, wu, wd, fg)


def kernel(x, c, w_mod, b_mod, norm1_g, w_in, ssm_a_re, ssm_a_im, ssm_log_dt, ssm_b_re, ssm_b_im, ssm_c_re, ssm_c_im, ssm_d, w_ssm_glu, b_ssm_glu, kv_norm_g, idx_k_norm_g, w_uk, w_uv, w_attn_proj, w_out, norm2_g, w_ffn_gate, w_ffn_up, w_ffn_down, final_g):
    bsz, seq, d = x.shape
    assert w_mod.shape[0] == 1, "single-layer block"
    assert seq % TOKEN_TILE == 0 and seq % (4 * KEY_TILE) == 0 and TOKEN_TILE % KEY_TILE == 0
    g, p = ssm_a_re.shape[1:]
    cg = ssm_b_re.shape[3]
    assert cg == SSM_GROUP and g % 2 == 0
    n_xs = g * cg
    n_kv, n_heads, d_head = w_uk.shape[1:]
    n_q = n_heads * d_head
    n_ki = idx_k_norm_g.shape[1]
    n_wi = w_in.shape[2] - (n_xs + n_q + n_kv + n_ki + 2 * d)
    n_wi = n_wi // (n_ki + 1)
    n_qi = n_wi * n_ki
    topk = min(TOPK_MAX, seq // 4)

    mod = _modulation(c, w_mod[0], b_mod[0])
    shift1, scale1, gate1, shift2, scale2, gate2 = [m.reshape(bsz, 1, d) for m in jnp.split(mod, 6, axis=-1)]

    offs = {}
    o = 0
    for name, size in (("xs", n_xs), ("q", n_q), ("ckv", n_kv), ("qi", n_qi), ("ki", n_ki), ("wi", n_wi),
                       ("ga", d), ("gb", d)):
        offs[name] = (o, o + size)
        o += size
    assert o == w_in.shape[2]
    wi_all = w_in[0]
    cols = lambda n: wi_all[:, offs[n][0]:offs[n][1]]
    wi_rows = -(-n_wi // 8) * 8
    w_nat = jnp.concatenate([cols("xs"), cols("ga"), cols("gb"), cols("ckv"), cols("ki")], axis=1).astype(BF16)
    w_tr = jnp.concatenate([cols("q"), cols("qi"), cols("ckv"),
                            jnp.pad(cols("wi"), ((0, 0), (0, wi_rows - n_wi)))], axis=1).T.astype(BF16)
    xs, ga, gb, ckv, ki, qT, qiT, ckvT, wiT = _in_projection(
        x, scale1, shift1, norm1_g[0], w_nat, w_tr, kv_norm_g[0], idx_k_norm_g[0],
        (n_xs, d, n_kv, n_ki, n_q, n_qi, wi_rows))

    prep = _ssm_prep(ssm_a_re[0], ssm_a_im[0], ssm_log_dt[0], ssm_b_re[0], ssm_b_im[0], ssm_c_re[0], ssm_c_im[0])
    y_gelu = _s5_branch(xs, _ssm_matrices(*prep, ssm_d[0]))

    y_att = _dsa_branch(qT, qiT, wiT, ki, ckv, ckvT, w_uk[0], w_uv[0], w_attn_proj[0], topk)

    return _merge_ffn(x, y_gelu, y_att, ga, gb, gate1, shift2, scale2, gate2,
                      w_ssm_glu[0], b_ssm_glu[0], w_out[0], norm2_g[0],
                      w_ffn_gate[0], w_ffn_up[0], w_ffn_down[0], final_g)
```

```python
import functools

import jax
import jax.numpy as jnp
from jax import lax
from jax.experimental import pallas as pl
from jax.experimental.pallas import tpu as pltpu

F32 = jnp.float32
BF16 = jnp.bfloat16
HIGHEST = lax.Precision.HIGHEST

EPS = 1e-6
SSM_GROUP = 16
SSM_CHUNK = 16
TOPK_MAX = 256
LANES = 128
MXU_TILE = 256
Q_BLOCK = 128
KEY_TILE = 256
ATT_STEP_TILES = 1
TOKEN_TILE = 512
ONES_ROWS = 8
INT_MIN = -(2 ** 31)
MASK_VALUE = -2.0 ** 100
NEG_INIT = -2.0 ** 99
LOG2E = 1.4426950408889634
VMEM_LIMIT = 56 * 1024 * 1024


def _rms(x, axis):
    return x * lax.rsqrt(jnp.mean(x * x, axis=axis, keepdims=True) + EPS)


def _mod_kernel(c_ref, w_ref, b_ref, o_ref):
    cv = c_ref[...]
    cond = cv * jax.nn.sigmoid(cv)
    o_ref[...] = jnp.dot(cond, w_ref[...], precision=HIGHEST, preferred_element_type=F32) + b_ref[...]


def _modulation(c, w_mod, b_mod):
    bsz, d = c.shape
    n = w_mod.shape[1]
    rows = -(-bsz // 8) * 8
    cp = jnp.pad(c, ((0, rows - bsz), (0, 0)))
    tn = d
    out = pl.pallas_call(
        _mod_kernel,
        grid=(n // tn,),
        in_specs=[pl.BlockSpec((rows, d), lambda j: (0, 0)),
                  pl.BlockSpec((d, tn), lambda j: (0, j)),
                  pl.BlockSpec((1, tn), lambda j: (0, j))],
        out_specs=pl.BlockSpec((rows, tn), lambda j: (0, j)),
        out_shape=jax.ShapeDtypeStruct((rows, n), F32),
        name="modulation",
    )(cp, w_mod, b_mod.reshape(1, n))
    return out[:bsz]


def _inproj_kernel(x_ref, sc_ref, sh_ref, g1_ref, wn_ref, wt_ref, gkv_ref, gkvc_ref, gki_ref,
                   xs_ref, ga_ref, gb_ref, ckv_ref, ki_ref, qT_ref, qiT_ref, ckvT_ref, wiT_ref,
                   *, n_xs, n_gate, n_kv, n_ki, n_q, n_qi, n_wi):
    x = x_ref[0]
    u = _rms(x, -1) * g1_ref[...]
    u = u * (1.0 + sc_ref[0]) + sh_ref[0]
    ub = u.astype(BF16)
    tm = ub.shape[0]

    zn = jnp.dot(ub, wn_ref[...], preferred_element_type=F32)
    o = 0
    for s in range(n_xs // LANES):
        xs_ref[0, s] = zn[:, s * LANES:(s + 1) * LANES]
    o += n_xs
    ga_ref[0] = zn[:, o:o + n_gate].astype(BF16)
    o += n_gate
    gb_ref[0] = zn[:, o:o + n_gate].astype(BF16)
    o += n_gate
    ckv_ref[0] = (_rms(zn[:, o:o + n_kv], -1) * gkv_ref[...]).astype(BF16)
    o += n_kv
    ki_ref[0] = (_rms(zn[:, o:o + n_ki], -1) * gki_ref[...]).astype(BF16)

    zt = lax.dot_general(wt_ref[...], ub, (((1,), (1,)), ((), ())), preferred_element_type=F32)
    o = 0
    qT_ref[0] = zt[o:o + n_q].astype(BF16)
    o += n_q
    qiT_ref[0] = zt[o:o + n_qi].astype(BF16)
    o += n_qi
    ct = (_rms(zt[o:o + n_kv], 0) * gkvc_ref[...]).astype(BF16)
    o += n_kv
    wiT_ref[0] = zt[o:o + n_wi]
    ones_blk = jnp.where(lax.broadcasted_iota(jnp.int32, (ONES_ROWS, KEY_TILE), 0) == 0, 1.0, 0.0).astype(BF16)
    for i in range(tm // KEY_TILE):
        ckvT_ref[0, i, 0:n_kv, :] = ct[:, i * KEY_TILE:(i + 1) * KEY_TILE]
        ckvT_ref[0, i, n_kv:n_kv + ONES_ROWS, :] = ones_blk


def _in_projection(x, scale1, shift1, norm1_g, w_nat, w_tr, kv_g, ki_g, sizes):
    bsz, seq, d = x.shape
    n_xs, n_gate, n_kv, n_ki, n_q, n_qi, n_wi = sizes
    tm = TOKEN_TILE
    nt = seq // tm
    kern = functools.partial(_inproj_kernel, n_xs=n_xs, n_gate=n_gate, n_kv=n_kv, n_ki=n_ki,
                             n_q=n_q, n_qi=n_qi, n_wi=n_wi)
    tok = lambda w: pl.BlockSpec((1, tm, w), lambda b, t: (b, t, 0))
    lane = lambda r: pl.BlockSpec((1, r, tm), lambda b, t: (b, 0, t))
    vec = lambda w: pl.BlockSpec((1, 1, w), lambda b, t: (b, 0, 0))
    full = lambda a: pl.BlockSpec(a.shape, lambda b, t: (0,) * a.ndim)
    g1 = norm1_g.reshape(1, d)
    gkv = kv_g.reshape(1, n_kv)
    gkvc = kv_g.reshape(n_kv, 1)
    gki = ki_g.reshape(1, n_ki)
    out_shape = (
        jax.ShapeDtypeStruct((bsz, n_xs // LANES, seq, LANES), F32),
        jax.ShapeDtypeStruct((bsz, seq, n_gate), BF16),
        jax.ShapeDtypeStruct((bsz, seq, n_gate), BF16),
        jax.ShapeDtypeStruct((bsz, seq, n_kv), BF16),
        jax.ShapeDtypeStruct((bsz, seq, n_ki), BF16),
        jax.ShapeDtypeStruct((bsz, n_q, seq), BF16),
        jax.ShapeDtypeStruct((bsz, n_qi, seq), BF16),
        jax.ShapeDtypeStruct((bsz, seq // KEY_TILE, n_kv + ONES_ROWS, KEY_TILE), BF16),
        jax.ShapeDtypeStruct((bsz, n_wi, seq), F32),
    )
    out_specs = (
        pl.BlockSpec((1, n_xs // LANES, tm, LANES), lambda b, t: (b, 0, t, 0)),
        tok(n_gate), tok(n_gate), tok(n_kv), tok(n_ki),
        lane(n_q), lane(n_qi),
        pl.BlockSpec((1, tm // KEY_TILE, n_kv + ONES_ROWS, KEY_TILE), lambda b, t: (b, t, 0, 0)),
        lane(n_wi),
    )
    return pl.pallas_call(
        kern,
        grid=(bsz, nt),
        in_specs=[tok(d), vec(d), vec(d), full(g1), full(w_nat), full(w_tr), full(gkv), full(gkvc), full(gki)],
        out_specs=out_specs,
        out_shape=out_shape,
        compiler_params=pltpu.CompilerParams(vmem_limit_bytes=VMEM_LIMIT),
        name="in_projection",
    )(x, scale1, shift1, g1, w_nat, w_tr, gkv, gkvc, gki)


def _ssm_prep_kernel(are_ref, aim_ref, ldt_ref, bre_ref, bim_ref, cre_ref, cim_ref, ctre_ref, ctim_ref,
                     kall_ref, dre_ref, dim_ref, vre_ref, vimn_ref, lre_ref, lim_ref):
    for gi in range(are_ref.shape[0]):
        _ssm_prep_group(gi, are_ref, aim_ref, ldt_ref, bre_ref, bim_ref, cre_ref, cim_ref, ctre_ref, ctim_ref,
                        kall_ref, dre_ref, dim_ref, vre_ref, vimn_ref, lre_ref, lim_ref)


def _ssm_prep_group(gi, are_ref, aim_ref, ldt_ref, bre_ref, bim_ref, cre_ref, cim_ref, ctre_ref, ctim_ref,
                    kall_ref, dre_ref, dim_ref, vre_ref, vimn_ref, lre_ref, lim_ref):
    ar = are_ref[gi]
    ai = aim_ref[gi]
    dt = jnp.exp(ldt_ref[gi])
    ard = ar * dt
    aid = ai * dt

    def lam_pow(kf):
        mag = jnp.exp(ard * kf)
        ang = aid * kf
        return mag * jnp.cos(ang), mag * jnp.sin(ang)

    lb_re, lb_im = lam_pow(1.0)
    den = ar * ar + ai * ai
    coef_re = ((lb_re - 1.0) * ar + lb_im * ai) / den
    coef_im = (lb_im * ar - (lb_re - 1.0) * ai) / den
    bre = bre_ref[gi]
    bim = bim_ref[gi]
    bb_re = coef_re * bre - coef_im * bim
    bb_im = coef_re * bim + coef_im * bre

    width = bre.shape[1]
    lag = lax.shift_right_logical(lax.broadcasted_iota(jnp.int32, (1, width), 1), 4).astype(F32)
    lk_re, lk_im = lam_pow(lag)
    d_re = lk_re * bb_re - lk_im * bb_im
    d_im = lk_re * bb_im + lk_im * bb_re
    dre_ref[gi] = d_re
    dim_ref[gi] = d_im
    kall_ref[gi] = (jnp.dot(cre_ref[gi], d_re, precision=HIGHEST, preferred_element_type=F32)
                    - jnp.dot(cim_ref[gi], d_im, precision=HIGHEST, preferred_element_type=F32))
    l1_re, l1_im = lam_pow(lag + 1.0)
    ctre = ctre_ref[gi]
    ctim = ctim_ref[gi]
    vre_ref[gi] = l1_re * ctre - l1_im * ctim
    vimn_ref[gi] = -(l1_re * ctim + l1_im * ctre)
    lt_re, lt_im = lam_pow(float(SSM_CHUNK))
    lre_ref[gi] = lt_re
    lim_ref[gi] = lt_im


def _ssm_prep(a_re, a_im, log_dt, b_re, b_im, c_re, c_im):
    g, p = a_re.shape
    cg = b_re.shape[2]
    width = SSM_CHUNK * cg
    col = lambda a: a.reshape(g, p, 1)
    b_t = lambda b: jnp.tile(b, (1, 1, SSM_CHUNK))
    c_t = lambda cc: jnp.tile(jnp.swapaxes(cc, 1, 2), (1, 1, SSM_CHUNK))
    gpb = LANES // cg
    blk = lambda s: pl.BlockSpec((gpb,) + s, lambda i: (i, 0, 0))
    outs = pl.pallas_call(
        _ssm_prep_kernel,
        grid=(g // gpb,),
        in_specs=[blk((p, 1)), blk((p, 1)), blk((1, 1)), blk((p, width)), blk((p, width)),
                  blk((cg, p)), blk((cg, p)), blk((p, width)), blk((p, width))],
        out_specs=(blk((cg, width)), blk((p, width)), blk((p, width)), blk((p, width)), blk((p, width)),
                   blk((p, 1)), blk((p, 1))),
        out_shape=(jax.ShapeDtypeStruct((g, cg, width), F32),) + (jax.ShapeDtypeStruct((g, p, width), F32),) * 4
        + (jax.ShapeDtypeStruct((g, p, 1), F32),) * 2,
        name="ssm_prep",
    )(col(a_re), col(a_im), log_dt.reshape(g, 1, 1), b_t(b_re), b_t(b_im), c_re, c_im, c_t(c_re), c_t(c_im))
    return outs


def _ssm_matrices(kall, d_re, d_im, v_re, v_imn, l_re, l_im, d_skip):
    g, cg, width = kall.shape
    p = d_re.shape[1]
    t = SSM_CHUNK
    gs = LANES // cg
    ns = g // gs
    kflip = kall.reshape(g, cg, t, cg)[:, :, ::-1, :].reshape(g, cg, width)
    kz = jnp.pad(kflip, ((0, 0), (0, 0), (0, width)))
    m = jnp.stack([kz[:, :, cg * (t - 1 - ti):cg * (t - 1 - ti) + width] for ti in range(t)], axis=1)
    m = m.reshape(ns, gs, width, width).astype(BF16)
    w_of = lambda d: d.reshape(g, p, t, cg)[:, :, ::-1, :].reshape(g, p, width)
    w = jnp.concatenate([w_of(d_re), w_of(d_im)], axis=1).reshape(ns, gs, 2 * p, width).astype(BF16)
    v = jnp.concatenate([jnp.swapaxes(v_re, 1, 2), jnp.swapaxes(v_imn, 1, 2)], axis=2)
    v = v.reshape(ns, gs, width, 2 * p).astype(BF16)
    lam = jnp.concatenate([l_re.reshape(ns, 1, gs * p), l_im.reshape(ns, 1, gs * p)], axis=1)
    dsk = jnp.broadcast_to(d_skip.reshape(g, 1, cg), (g, t, cg)).reshape(ns, gs, width, 1)
    return m, w, v, lam, dsk


def _gelu_tanh(y):
    return 0.5 * y * (1.0 + jnp.tanh(0.7978845608028654 * (y + 0.044715 * (y * y * y))))


def _ssm_kernel(xs_ref, m_ref, w_ref, v_ref, lam_ref, d_ref, y_ref,
                zt_ref, ure_ref, uim_ref, sre_ref, sim_ref, yt_ref):
    t = SSM_CHUNK
    cg = SSM_GROUP
    gs = zt_ref.shape[0]
    nch = zt_ref.shape[2]
    p = w_ref.shape[2] // 2

    for ti in range(t):
        xt = xs_ref[0, 0, pl.ds(ti, nch, stride=t), :].T
        for g in range(gs):
            zt_ref[g, ti * cg:(ti + 1) * cg, :] = xt[g * cg:(g + 1) * cg, :].astype(BF16)

    ut = [jnp.dot(w_ref[0, g], zt_ref[g], preferred_element_type=F32) for g in range(gs)]
    ure_ref[...] = jnp.concatenate([u[:p] for u in ut], axis=0).T
    uim_ref[...] = jnp.concatenate([u[p:] for u in ut], axis=0).T

    lr = lam_ref[0, 0:1, :]
    li = lam_ref[0, 1:2, :]

    def scan_body(i, carry):
        s_re, s_im = carry
        r0 = pl.multiple_of(i * 8, 8)
        u_re = ure_ref[pl.ds(r0, 8), :]
        u_im = uim_ref[pl.ds(r0, 8), :]
        prev_re, prev_im = [], []
        for r in range(8):
            prev_re.append(s_re)
            prev_im.append(s_im)
            s_re, s_im = (lr * s_re - li * s_im + u_re[r:r + 1, :], lr * s_im + li * s_re + u_im[r:r + 1, :])
        sre_ref[pl.ds(r0, 8), :] = jnp.concatenate(prev_re, axis=0)
        sim_ref[pl.ds(r0, 8), :] = jnp.concatenate(prev_im, axis=0)
        return s_re, s_im

    zero = jnp.zeros_like(lr)
    lax.fori_loop(0, nch // 8, scan_body, (zero, zero))

    sre_t = sre_ref[...].T
    sim_t = sim_ref[...].T
    for g in range(gs):
        z = zt_ref[g]
        s = jnp.concatenate([sre_t[g * p:(g + 1) * p], sim_t[g * p:(g + 1) * p]], axis=0).astype(BF16)
        y = (jnp.dot(m_ref[0, g], z, preferred_element_type=F32)
             + jnp.dot(v_ref[0, g], s, preferred_element_type=F32)
             + d_ref[0, g] * z.astype(F32))
        yt_ref[g] = _gelu_tanh(y)

    for ti in range(t):
        blk = jnp.concatenate([yt_ref[g, ti * cg:(ti + 1) * cg, :] for g in range(gs)], axis=0)
        y_ref[0, 0, pl.ds(ti, nch, stride=t), :] = blk.T


def _s5_branch(xs, mats):
    m, w, v, lam, dsk = mats
    bsz, ns, seq, lanes = xs.shape
    gs = m.shape[1]
    width = m.shape[2]
    p2 = w.shape[2]
    nch = seq // SSM_CHUNK
    slab = pl.BlockSpec((1, 1, seq, lanes), lambda b, s: (b, s, 0, 0))
    per_slab = lambda a: pl.BlockSpec((1,) + a.shape[1:], lambda b, s: (s,) + (0,) * (a.ndim - 1))
    return pl.pallas_call(
        _ssm_kernel,
        grid=(bsz, ns),
        in_specs=[slab, per_slab(m), per_slab(w), per_slab(v), per_slab(lam), per_slab(dsk)],
        out_specs=slab,
        out_shape=jax.ShapeDtypeStruct(xs.shape, F32),
        scratch_shapes=[pltpu.VMEM((gs, width, nch), BF16),
                        pltpu.VMEM((nch, gs * p2 // 2), F32), pltpu.VMEM((nch, gs * p2 // 2), F32),
                        pltpu.VMEM((nch, gs * p2 // 2), F32), pltpu.VMEM((nch, gs * p2 // 2), F32),
                        pltpu.VMEM((gs, width, nch), F32)],
        compiler_params=pltpu.CompilerParams(vmem_limit_bytes=VMEM_LIMIT),
        name="ssm",
    )(xs, m, w, v, lam, dsk)


def _i32(v):
    return jnp.int32(v - (1 << 32) if v >= (1 << 31) else v)


def _order_pattern(x):
    bits = pltpu.bitcast(x, jnp.int32)
    return bits ^ (lax.shift_right_arithmetic(bits, 31) | jnp.int32(INT_MIN))


def _pattern_to_float(u):
    bits = jnp.where(u < 0, u & jnp.int32(0x7FFFFFFF), ~u)
    return pltpu.bitcast(bits, F32)


def _finite_threshold(u):
    return _pattern_to_float(jnp.where(u < 0, u, jnp.maximum(u, jnp.int32(0x00800000))))


def _bit_transpose32(words):
    a = list(words)
    j, m = 16, 0x0000FFFF
    while j:
        k = 0
        while k < 32:
            t = (a[k] ^ lax.shift_right_logical(a[k + j], j)) & _i32(m)
            a[k] = a[k] ^ t
            a[k + j] = a[k + j] ^ lax.shift_left(t, j)
            k = (k + j + 1) & ~j
        j >>= 1
        m = (m ^ (m << j)) & 0xFFFFFFFF
    return a


def _dsa_kernel(qT_ref, qiT_ref, wiT_ref, ki_ref, ckv_ref, ckvT_ref, wuk_ref, wuvT_ref, wproj_ref,
                o_ref, scores_ref, planes_ref, cand_ref, sel_ref, mask_ref, wq_ref, acc_ref, m_ref,
                sa_ref, sb_ref,
                *, topk, n_heads, d_head, n_iheads, d_ihead, n_kv):
    qb = Q_BLOCK
    tk = KEY_TILE
    words = tk // 8
    j = pl.program_id(1)
    n_tiles = (j * qb + qb + tk - 1) // tk

    for h in range(n_heads):
        qh = qT_ref[0, h * d_head:(h + 1) * d_head, :]
        ql = jnp.dot(wuk_ref[h], qh, preferred_element_type=F32) * (d_head ** -0.5 * LOG2E)
        wq_ref[0:n_kv, h * qb:(h + 1) * qb] = ql.astype(BF16)
    qi_cat = jnp.concatenate([qiT_ref[0, h * d_ihead:(h + 1) * d_ihead, :] for h in range(n_iheads)], axis=1)
    w_heads = wiT_ref[0] * (n_iheads ** -0.5 * d_ihead ** -0.5)

    q_pos = j * qb + lax.broadcasted_iota(jnp.int32, (tk, qb), 1)
    row_iota = lax.broadcasted_iota(jnp.int32, (tk, qb), 0)

    def score_tile(kt, diagonal):
        k0 = pl.multiple_of(kt * tk, tk)
        rel = jnp.dot(ki_ref[0, pl.ds(k0, tk), :], qi_cat, preferred_element_type=F32)
        sc = None
        for h in range(n_iheads):
            term = jnp.maximum(rel[:, h * qb:(h + 1) * qb], 0.0) * w_heads[h:h + 1, :]
            sc = term if sc is None else sc + term
        if diagonal:
            sc = jnp.where((k0 + row_iota) <= q_pos, sc, -jnp.inf)
        scores_ref[kt] = sc
        bits = pltpu.bitcast(sc, jnp.int32)
        planes = _bit_transpose32([bits[8 * v:8 * v + 8] for v in range(words)])
        sign = planes[0]
        planes_ref[kt, 0] = ~sign
        for p in range(1, 32):
            planes_ref[kt, p] = planes[p] ^ sign

    @pl.when(jnp.logical_and(pl.program_id(0) == 0, j == 0))
    def _():
        planes_ref[...] = jnp.zeros(planes_ref.shape, jnp.int32)

    def score_pair(i):
        score_tile(2 * i, False)
        score_tile(2 * i + 1, False)

    def score_body(i, carry):
        for r in range(4):
            score_pair(4 * i + r)
        return carry

    last_pair = (n_tiles + 1) // 2 - 1
    lax.fori_loop(0, last_pair // 4, score_body, 0)
    done_pairs = (last_pair // 4) * 4

    @pl.when(last_pair - done_pairs >= 2)
    def _():
        score_pair(done_pairs)
        score_pair(done_pairs + 1)

    @pl.when(last_pair % 2 == 1)
    def _():
        score_pair(last_pair - 1)

    score_tile(2 * last_pair, True)
    score_tile(2 * last_pair + 1, True)

    def radix_select(nt):
        active = lax.broadcasted_iota(jnp.int32, (nt, 8, qb), 0) < n_tiles
        cand_ref[0:nt] = jnp.where(active, jnp.int32(-1), jnp.int32(0))

        def total(words):
            return jnp.sum(jnp.sum(lax.population_count(words), axis=0), axis=0, keepdims=True)

        def radix_body(i, carry):
            k_rem, pat = carry
            hi_plane = planes_ref[0:nt, 2 * i]
            lo_plane = planes_ref[0:nt, 2 * i + 1]
            cand = cand_ref[0:nt]
            with_hi = cand & hi_plane
            without_hi = cand ^ with_hi
            n_hi = total(with_hi)
            n_hi_lo = total(with_hi & lo_plane)
            n_lo = total(without_hi & lo_plane)
            take_hi = n_hi >= k_rem
            k_mid = jnp.where(take_hi, k_rem, k_rem - n_hi)
            n_next = jnp.where(take_hi, n_hi_lo, n_lo)
            take_lo = n_next >= k_mid
            keep = jnp.where(take_hi, with_hi, without_hi)
            cand_ref[0:nt] = keep & (lo_plane ^ jnp.where(take_lo, jnp.int32(0), jnp.int32(-1)))
            hi_bit = lax.shift_left(jnp.int32(1), 31 - 2 * i)
            lo_bit = lax.shift_left(jnp.int32(1), 30 - 2 * i)
            pat = jnp.where(take_hi, pat | hi_bit, pat)
            pat = jnp.where(take_lo, pat | lo_bit, pat)
            return jnp.where(take_lo, k_mid, k_mid - n_next), pat

        _, pat = lax.fori_loop(0, 16, radix_body,
                               (jnp.full((1, qb), topk, jnp.int32), jnp.zeros((1, qb), jnp.int32)))
        sel_ref[3:4, :] = pltpu.bitcast(pat, F32)

    quarter = cand_ref.shape[0] // 4
    for v in range(1, 5):
        @pl.when(jnp.logical_and(n_tiles > (v - 1) * quarter, n_tiles <= v * quarter))
        def _(v=v):
            radix_select(v * quarter)

    t_u = pltpu.bitcast(sel_ref[3:4, :], jnp.int32)

    tile_pairs = (n_tiles + 1) // 2

    def publish(thr):
        def body(i, carry):
            n_gt, n_ge = carry
            for kt in (2 * i, 2 * i + 1):
                x = scores_ref[kt]
                ge = x >= thr
                mask_ref[kt] = jnp.where(ge, 0.0, MASK_VALUE).astype(BF16)
                n_gt = n_gt + jnp.sum(jnp.where(x > thr, 1.0, 0.0), axis=0, keepdims=True)
                n_ge = n_ge + jnp.sum(jnp.where(ge, 1.0, 0.0), axis=0, keepdims=True)
            return n_gt, n_ge
        zero = jnp.zeros((1, qb), F32)
        n_gt, n_ge = lax.fori_loop(0, tile_pairs, body, (zero, zero))
        sel_ref[0:1, :] = thr
        sel_ref[1:2, :] = n_gt
        sel_ref[2:3, :] = n_ge
        return n_gt, n_ge

    n_gt, n_ge = publish(_finite_threshold(t_u))
    few = (j * qb + lax.broadcasted_iota(jnp.int32, (1, qb), 1)) < topk - 1
    good = jnp.logical_or(few, jnp.logical_and(n_gt < topk, n_ge >= topk))
    all_good = jnp.min(jnp.where(good, 1, 0))

    @pl.when(all_good == 0)
    def _():
        def bit_body(i, pat_u):
            cand_u = pat_u | lax.shift_left(jnp.int32(1), 31 - i)
            cand_f = _pattern_to_float(cand_u)

            def cnt_body(kt, acc):
                return acc + jnp.sum(jnp.where(scores_ref[kt] >= cand_f, 1.0, 0.0), axis=0, keepdims=True)

            cnt = lax.fori_loop(0, n_tiles, cnt_body, jnp.zeros((1, qb), F32))
            return jnp.where(cnt >= topk, cand_u, pat_u)

        pat_u = lax.fori_loop(0, 32, bit_body, jnp.zeros((1, qb), jnp.int32))
        publish(_finite_threshold(pat_u))

    thr = sel_ref[0:1, :]
    need = topk - sel_ref[1:2, :]
    most_ge = jnp.max(sel_ref[2:3, :])

    @pl.when(most_ge > topk)
    def _():
        lower = (lax.broadcasted_iota(jnp.int32, (tk, tk), 0) >= lax.broadcasted_iota(jnp.int32, (tk, tk), 1))
        lower = jnp.where(lower, 1.0, 0.0).astype(BF16)

        def bias_body(kt, seen):
            x = scores_ref[kt]
            eq = x == thr
            eqf = jnp.where(eq, 1.0, 0.0).astype(BF16)
            rank = jnp.dot(lower, eqf, preferred_element_type=F32) + seen
            keep_tie = jnp.where(rank <= need, 0.0, MASK_VALUE)
            mask_ref[kt] = jnp.where(x > thr, 0.0, jnp.where(eq, keep_tie, MASK_VALUE)).astype(BF16)
            return rank[tk - 1:tk, :]

        lax.fori_loop(0, n_tiles, bias_body, jnp.zeros((1, qb), F32))

    m_ref[...] = jnp.full(m_ref.shape, NEG_INIT, F32)
    acc_ref[...] = jnp.zeros(acc_ref.shape, F32)
    empty_tile = 2 * tile_pairs
    mask_ref[empty_tile] = jnp.full((tk, qb), MASK_VALUE, BF16)
    max_tile = ckvT_ref.shape[1] - 1
    col_q = lax.broadcasted_iota(jnp.int32, (qb, n_heads * qb), 1) & (qb - 1)
    row_q = lax.broadcasted_iota(jnp.int32, (qb, n_heads * qb), 0)
    wq_ref[n_kv:n_kv + qb, :] = jnp.where(col_q == row_q, 1.0, 0.0).astype(BF16)

    pair = 2 * qb
    n_pairs = n_heads // 2
    spt = ATT_STEP_TILES
    n_steps = (n_tiles + spt - 1) // spt

    def key_rows(kt):
        k0 = pl.multiple_of(jnp.minimum(kt, max_tile) * tk, tk)
        return jnp.concatenate([ckv_ref[0, pl.ds(k0, tk), :], mask_ref[jnp.minimum(kt, empty_tile)]], axis=1)

    def produce(st, s_ref):
        lhs = jnp.concatenate([key_rows(spt * st + r) for r in range(spt)], axis=0)
        for hp in range(n_pairs):
            sl = slice(hp * pair, (hp + 1) * pair)
            s_ref[:, sl] = jnp.dot(lhs, wq_ref[:, sl], preferred_element_type=F32)

    def consume(st, s_ref):
        cv = jnp.concatenate([ckvT_ref[0, jnp.minimum(spt * st + r, max_tile)] for r in range(spt)], axis=1)
        for hp in range(n_pairs):
            sl = slice(hp * pair, (hp + 1) * pair)
            m_old = m_ref[:, sl]
            m_new = jnp.maximum(m_old, jnp.max(s_ref[:, sl], axis=0, keepdims=True))
            p = jnp.exp2(s_ref[:, sl] - m_new).astype(BF16)
            acc_ref[:, sl] = acc_ref[:, sl] * jnp.exp2(m_old - m_new) + jnp.dot(
                cv, p, preferred_element_type=F32)
            m_ref[:, sl] = m_new

    produce(0, sa_ref)

    def two_steps(base):
        produce(base + 1, sb_ref)
        consume(base, sa_ref)
        produce(base + 2, sa_ref)
        consume(base + 1, sb_ref)

    def sixteen_steps(i, carry):
        for r in range(8):
            two_steps(16 * i + 2 * r)
        return carry

    lax.fori_loop(0, n_steps // 16, sixteen_steps, 0)
    done16 = (n_steps // 16) * 16

    @pl.when(n_steps - done16 >= 8)
    def _():
        for r in range(4):
            two_steps(done16 + 2 * r)

    done = (n_steps // 8) * 8

    @pl.when(n_steps - done >= 4)
    def _():
        two_steps(done)
        two_steps(done + 2)

    done4 = (n_steps // 4) * 4

    @pl.when(n_steps - done4 >= 2)
    def _():
        two_steps(done4)

    @pl.when(n_steps % 2 == 1)
    def _():
        consume(n_steps - 1, sa_ref)

    inv_l = 1.0 / acc_ref[n_kv:n_kv + 1, :]
    parts = []
    for h in range(n_heads):
        o_h = (acc_ref[0:n_kv, h * qb:(h + 1) * qb] * inv_l[:, h * qb:(h + 1) * qb]).astype(BF16)
        parts.append(jnp.dot(wuvT_ref[h], o_h, preferred_element_type=F32))
    t_tok = jnp.concatenate(parts, axis=0).T.astype(BF16)
    o_ref[0] = jnp.dot(t_tok, wproj_ref[...], preferred_element_type=F32).astype(BF16)


def _dsa_branch(qT, qiT, wiT, ki, ckv, ckvT, w_uk, w_uv, w_proj, topk):
    bsz, n_q, seq = qT.shape
    n_kv, n_heads, d_head = w_uk.shape
    n_ki = ki.shape[2]
    n_iheads = qiT.shape[1] // n_ki
    d_out = w_proj.shape[1]
    qb, tk = Q_BLOCK, KEY_TILE
    wuk = jnp.transpose(w_uk, (1, 0, 2)).astype(BF16)
    wuvT = jnp.transpose(w_uv, (1, 2, 0)).astype(BF16)
    wproj = w_proj.astype(BF16)
    kern = functools.partial(_dsa_kernel, topk=topk, n_heads=n_heads, d_head=d_head,
                             n_iheads=n_iheads, d_ihead=n_ki, n_kv=n_kv)
    lane = lambda r: pl.BlockSpec((1, r, qb), lambda b, j: (b, 0, j))
    per_b = lambda a: pl.BlockSpec((1,) + a.shape[1:], lambda b, j: (b,) + (0,) * (a.ndim - 1))
    full = lambda a: pl.BlockSpec(a.shape, lambda b, j: (0,) * a.ndim)
    return pl.pallas_call(
        kern,
        grid=(bsz, seq // qb),
        in_specs=[lane(n_q), lane(qiT.shape[1]), lane(wiT.shape[1]), per_b(ki), per_b(ckv), per_b(ckvT),
                  full(wuk), full(wuvT), full(wproj)],
        out_specs=pl.BlockSpec((1, qb, d_out), lambda b, j: (b, j, 0)),
        out_shape=jax.ShapeDtypeStruct((bsz, seq, d_out), BF16),
        scratch_shapes=[pltpu.VMEM((seq // tk, tk, qb), F32),
                        pltpu.VMEM((seq // tk, 33, 8, qb), jnp.int32),
                        pltpu.VMEM((seq // tk, 8, qb), jnp.int32),
                        pltpu.VMEM((8, qb), F32),
                        pltpu.VMEM((seq // tk + 1, tk, qb), BF16),
                        pltpu.VMEM((n_kv + qb, n_heads * qb), BF16),
                        pltpu.VMEM((n_kv + ONES_ROWS, n_heads * qb), F32),
                        pltpu.VMEM((1, n_heads * qb), F32),
                        pltpu.VMEM((ATT_STEP_TILES * tk, n_heads * qb), F32),
                        pltpu.VMEM((ATT_STEP_TILES * tk, n_heads * qb), F32)],
        compiler_params=pltpu.CompilerParams(vmem_limit_bytes=VMEM_LIMIT),
        name="dsa",
    )(qT, qiT, wiT, ki, ckv, ckvT, wuk, wuvT, wproj)


def _ffn_kernel(x_ref, y_ref, a_ref, ga_ref, gb_ref, g1_ref, sh2_ref, sc2_ref, g2_ref,
                wglu_ref, bglu_ref, wout_ref, n2_ref, wg_ref, wu_ref, wd_ref, fg_ref, o_ref, *, bounds):
    x = x_ref[0]
    y_gelu = jnp.concatenate([y_ref[0, s] for s in range(y_ref.shape[1])], axis=1).astype(BF16)
    z = jnp.dot(y_gelu, wglu_ref[...], preferred_element_type=F32) + bglu_ref[...]
    d = z.shape[1] // 2
    y_ssm = z[:, :d] * jax.nn.sigmoid(z[:, d:])
    merged = (jax.nn.sigmoid(ga_ref[0].astype(F32)) * y_ssm
              + jax.nn.sigmoid(gb_ref[0].astype(F32)) * a_ref[0].astype(F32))
    h1 = x + g1_ref[0] * jnp.dot(merged.astype(BF16), wout_ref[...], preferred_element_type=F32)
    u2 = _rms(h1, -1) * n2_ref[...]
    u2 = (u2 * (1.0 + sc2_ref[0]) + sh2_ref[0]).astype(BF16)
    ffn = None
    for lo, hi in zip(bounds[:-1], bounds[1:]):
        sl = slice(lo, hi)
        gate = jnp.dot(u2, wg_ref[:, sl], preferred_element_type=F32)
        up = jnp.dot(u2, wu_ref[:, sl], preferred_element_type=F32)
        act = ((gate * jax.nn.sigmoid(gate)) * up).astype(BF16)
        part = jnp.dot(act, wd_ref[sl, :], preferred_element_type=F32)
        ffn = part if ffn is None else ffn + part
    h2 = h1 + g2_ref[0] * ffn
    o_ref[0] = _rms(h2, -1) * fg_ref[...]


def _merge_ffn(x, y_gelu, y_att, ga, gb, gate1, shift2, scale2, gate2,
               w_glu, b_glu, w_out, norm2_g, w_gate, w_up, w_down, final_g):
    bsz, seq, d = x.shape
    tm = TOKEN_TILE
    hidden = w_gate.shape[1]
    bounds = (0, -(-hidden // (2 * MXU_TILE)) * MXU_TILE, hidden) if hidden > 2 * MXU_TILE else (0, hidden)
    tok = lambda w: pl.BlockSpec((1, tm, w), lambda b, t: (b, t, 0))
    vec = lambda w: pl.BlockSpec((1, 1, w), lambda b, t: (b, 0, 0))
    const = lambda a: pl.BlockSpec(a.shape, lambda b, t: (0,) * a.ndim, pipeline_mode=pl.Buffered(1))
    wglu = w_glu.astype(BF16)
    bglu = b_glu.reshape(1, -1)
    wout = w_out.astype(BF16)
    n2 = norm2_g.reshape(1, d)
    wg = w_gate.astype(BF16)
    wu = w_up.astype(BF16)
    wd = w_down.astype(BF16)
    fg = final_g.reshape(1, d)
    return pl.pallas_call(
        functools.partial(_ffn_kernel, bounds=bounds),
        grid=(bsz, seq // tm),
        in_specs=[tok(d), pl.BlockSpec((1, y_gelu.shape[1], tm, LANES), lambda b, t: (b, 0, t, 0)),
                  tok(d), tok(d), tok(d), vec(d), vec(d), vec(d), vec(d),
                  const(wglu), const(bglu), const(wout), const(n2), const(wg), const(wu), const(wd), const(fg)],
        out_specs=tok(d),
        out_shape=jax.ShapeDtypeStruct((bsz, seq, d), F32),
        compiler_params=pltpu.CompilerParams(vmem_limit_bytes=VMEM_LIMIT),
        name="merge_ffn",
    )(x, y_gelu, y_att, ga, gb, gate1, shift2, scale2, gate2, wglu, bglu, wout, n2, wg, wu, wd, fg)


def kernel(x, c, w_mod, b_mod, norm1_g, w_in, ssm_a_re, ssm_a_im, ssm_log_dt, ssm_b_re, ssm_b_im, ssm_c_re, ssm_c_im, ssm_d, w_ssm_glu, b_ssm_glu, kv_norm_g, idx_k_norm_g, w_uk, w_uv, w_attn_proj, w_out, norm2_g, w_ffn_gate, w_ffn_up, w_ffn_down, final_g):
    bsz, seq, d = x.shape
    assert w_mod.shape[0] == 1, "single-layer block"
    assert seq % TOKEN_TILE == 0 and seq % (4 * KEY_TILE) == 0 and TOKEN_TILE % KEY_TILE == 0
    g, p = ssm_a_re.shape[1:]
    cg = ssm_b_re.shape[3]
    assert cg == SSM_GROUP and g % 2 == 0
    n_xs = g * cg
    n_kv, n_heads, d_head = w_uk.shape[1:]
    n_q = n_heads * d_head
    n_ki = idx_k_norm_g.shape[1]
    n_wi = w_in.shape[2] - (n_xs + n_q + n_kv + n_ki + 2 * d)
    n_wi = n_wi // (n_ki + 1)
    n_qi = n_wi * n_ki
    topk = min(TOPK_MAX, seq // 4)

    mod = _modulation(c, w_mod[0], b_mod[0])
    shift1, scale1, gate1, shift2, scale2, gate2 = [m.reshape(bsz, 1, d) for m in jnp.split(mod, 6, axis=-1)]

    offs = {}
    o = 0
    for name, size in (("xs", n_xs), ("q", n_q), ("ckv", n_kv), ("qi", n_qi), ("ki", n_ki), ("wi", n_wi),
                       ("ga", d), ("gb", d)):
        offs[name] = (o, o + size)
        o += size
    assert o == w_in.shape[2]
    wi_all = w_in[0]
    cols = lambda n: wi_all[:, offs[n][0]:offs[n][1]]
    wi_rows = -(-n_wi // 8) * 8
    w_nat = jnp.concatenate([cols("xs"), cols("ga"), cols("gb"), cols("ckv"), cols("ki")], axis=1).astype(BF16)
    w_tr = jnp.concatenate([cols("q"), cols("qi"), cols("ckv"),
                            jnp.pad(cols("wi"), ((0, 0), (0, wi_rows - n_wi)))], axis=1).T.astype(BF16)
    xs, ga, gb, ckv, ki, qT, qiT, ckvT, wiT = _in_projection(
        x, scale1, shift1, norm1_g[0], w_nat, w_tr, kv_norm_g[0], idx_k_norm_g[0],
        (n_xs, d, n_kv, n_ki, n_q, n_qi, wi_rows))

    prep = _ssm_prep(ssm_a_re[0], ssm_a_im[0], ssm_log_dt[0], ssm_b_re[0], ssm_b_im[0], ssm_c_re[0], ssm_c_im[0])
    y_gelu = _s5_branch(xs, _ssm_matrices(*prep, ssm_d[0]))

    y_att = _dsa_branch(qT, qiT, wiT, ki, ckv, ckvT, w_uk[0], w_uv[0], w_attn_proj[0], topk)

    return _merge_ffn(x, y_gelu, y_att, ga, gb, gate1, shift2, scale2, gate2,
                      w_ssm_glu[0], b_ssm_glu[0], w_out[0], norm2_g[0],
                      w_ffn_gate[0], w_ffn_up[0], w_ffn_down[0], final_g)
```
